```python
import math
import jax
import jax.numpy as jnp
from jax import lax
import numpy as np


D_MODEL = 1024
BATCH = 16
SEQ = 4096
DEPTH = 4

N_DIFF_HEADS = 8
DIFF_HEAD_DIM = D_MODEL // N_DIFF_HEADS // 2
ROPE_THETA = 10000.0
Q_BLOCK = 128
D_RNN = D_MODEL
N_LRU_BLOCKS = 8
LRU_BLOCK = D_RNN // N_LRU_BLOCKS
CONV_WIDTH = 4
LRU_C = 8.0
D_FF = 2816
N_EXPERTS = 8
TOP_K = 2
D_FF_EXPERT = 3584
EXPERT_BLOCK = 256
D_PLE = 256
NORM_EPS = 1e-6
SUBLN_EPS = 1e-5

kernel_name = 'hybrid_diffattn_rglru_moe'


def rmsnorm(x, g, eps=NORM_EPS):
    xf = x.astype(jnp.float32)
    y = xf * lax.rsqrt(jnp.mean(xf * xf, axis=-1, keepdims=True) + eps)
    return (y * g.astype(jnp.float32)).astype(x.dtype)


def rope_tables(seq, dim):
    inv = 1.0 / (ROPE_THETA ** (jnp.arange(0, dim, 2, dtype=jnp.float32) / dim))
    ang = jnp.arange(seq, dtype=jnp.float32)[:, None] * inv[None, :]
    return jnp.cos(ang), jnp.sin(ang)


def apply_rope(x, cos, sin):
    x1, x2 = jnp.split(x.astype(jnp.float32), 2, axis=-1)
    c = cos[None, :, None, :]
    s = sin[None, :, None, :]
    return jnp.concatenate([x1 * c - x2 * s, x2 * c + x1 * s], axis=-1).astype(x.dtype)


def diff_attention(xn, wq, wk, wv, wo, lq1, lk1, lq2, lk2, subln_g, lambda_init):
    B, S, _ = xn.shape
    H, d = N_DIFF_HEADS, DIFF_HEAD_DIM
    q = (xn @ wq).reshape(B, S, 2 * H, d)
    k = (xn @ wk).reshape(B, S, 2 * H, d)
    v = (xn @ wv).reshape(B, S, H, 2 * d)
    cos, sin = rope_tables(S, d)
    q = apply_rope(q, cos, sin) * (d ** -0.5)
    k = apply_rope(k, cos, sin)
    lam = (jnp.exp(jnp.sum(lq1.astype(jnp.float32) * lk1.astype(jnp.float32)))
           - jnp.exp(jnp.sum(lq2.astype(jnp.float32) * lk2.astype(jnp.float32)))
           + lambda_init)
    outs = []
    for start in range(0, S, Q_BLOCK):
        end = start + Q_BLOCK
        qb = q[:, start:end]
        kb = k[:, :end]
        vb = v[:, :end]
        s = jnp.einsum('bqmd,bkmd->bmqk', qb, kb, preferred_element_type=jnp.float32)
        qpos = jnp.arange(start, end)[:, None]
        kpos = jnp.arange(end)[None, :]
        s = jnp.where(kpos <= qpos, s, -jnp.inf)
        w = jax.nn.softmax(s, axis=-1).reshape(B, H, 2, Q_BLOCK, end)
        wdiff = w[:, :, 0] - lam * w[:, :, 1]
        outs.append(jnp.einsum('bhqk,bkhe->bqhe', wdiff.astype(vb.dtype), vb))
    o = jnp.concatenate(outs, axis=1)
    o = rmsnorm(o, subln_g, SUBLN_EPS) * (1.0 - lambda_init)
    return o.reshape(B, S, H * 2 * d) @ wo


def block_diag_linear(x, w, b):
    B, S, C = x.shape
    xb = x.reshape(B, S, N_LRU_BLOCKS, LRU_BLOCK)
    return jnp.einsum('bsnc,ncd->bsnd', xb, w).reshape(B, S, C) + b


def causal_depthwise_conv(x, w, b):
    C = x.shape[-1]
    y = lax.conv_general_dilated(
        x, w[:, None, :], window_strides=(1,), padding=[(CONV_WIDTH - 1, 0)],
        dimension_numbers=('NWC', 'WIO', 'NWC'), feature_group_count=C)
    return y + b


def rg_lru(x, wa, ba, wx, bx, lam):
    r = jax.nn.sigmoid(block_diag_linear(x, wa, ba)).astype(jnp.float32)
    i = jax.nn.sigmoid(block_diag_linear(x, wx, bx)).astype(jnp.float32)
    log_a = -LRU_C * r * jax.nn.softplus(-lam.astype(jnp.float32))
    a = jnp.exp(log_a)
    mult = jnp.sqrt(-jnp.expm1(2.0 * log_a))
    bt = mult * i * x.astype(jnp.float32)

    def combine(left, right):
        a1, b1 = left
        a2, b2 = right
        return a1 * a2, a2 * b1 + b2

    _, h = lax.associative_scan(combine, (a, bt), axis=1)
    return h.astype(x.dtype)


def recurrent_block(xn, w_in_x, w_in_y, conv_w, conv_b, wa, ba, wx, bx, lam, w_out):
    y = jax.nn.gelu(xn @ w_in_y, approximate=True)
    u = causal_depthwise_conv(xn @ w_in_x, conv_w, conv_b)
    h = rg_lru(u, wa, ba, wx, bx, lam)
    return (h * y) @ w_out


def swiglu(xn, wg, wu, wd):
    return (jax.nn.silu(xn @ wg) * (xn @ wu)) @ wd


def moe_swiglu(xn, w_router, w_gate, w_up, w_down):
    B, S, D = xn.shape
    N = B * S
    xt = xn.reshape(N, D)
    logits = jnp.einsum('nd,de->ne', xt, w_router, preferred_element_type=jnp.float32)
    top_vals, top_idx = lax.top_k(logits, TOP_K)
    gates = jax.nn.softmax(top_vals, axis=-1)
    flat_e = top_idx.reshape(-1).astype(jnp.int32)
    flat_tok = jnp.repeat(jnp.arange(N, dtype=jnp.int32), TOP_K)
    flat_g = gates.reshape(-1)
    order = jnp.argsort(flat_e, stable=True)
    sorted_e = flat_e[order]
    counts = jnp.bincount(flat_e, length=N_EXPERTS).astype(jnp.int32)
    padded = (counts + EXPERT_BLOCK - 1) // EXPERT_BLOCK * EXPERT_BLOCK
    start = jnp.cumsum(counts) - counts
    pstart = jnp.cumsum(padded) - padded
    rank = jnp.arange(N * TOP_K, dtype=jnp.int32) - start[sorted_e]
    dest = pstart[sorted_e] + rank
    n_blocks = -(-(N * TOP_K) // EXPERT_BLOCK) + N_EXPERTS
    P = n_blocks * EXPERT_BLOCK
    slot_tok = jnp.zeros((P,), jnp.int32).at[dest].set(flat_tok[order])
    slot_gate = jnp.zeros((P,), jnp.float32).at[dest].set(flat_g[order])
    block_start = jnp.arange(n_blocks, dtype=jnp.int32) * EXPERT_BLOCK
    block_expert = jnp.minimum(
        jnp.searchsorted(jnp.cumsum(padded), block_start, side='right'), N_EXPERTS - 1)
    xs = xt[slot_tok].reshape(n_blocks, EXPERT_BLOCK, D)

    def expert_block(args):
        xb, e = args
        hb = jax.nn.silu(xb @ w_gate[e]) * (xb @ w_up[e])
        return hb @ w_down[e]

    ys = lax.map(expert_block, (xs, block_expert)).reshape(P, D)
    out = jnp.zeros((N, D), jnp.float32).at[slot_tok].add(
        ys.astype(jnp.float32) * slot_gate[:, None])
    return out.astype(xn.dtype).reshape(B, S, D)


def per_layer_embedding(h, p_i, ln_g, w_proj, w_gate):
    g = jax.nn.sigmoid(rmsnorm(h, ln_g) @ w_gate)
    return g * (p_i @ w_proj)


def setup_inputs(seed: int = 0) -> dict:
    key = jax.random.key(seed)
    ks = jax.random.split(key, 40)
    n_even = (DEPTH + 1) // 2
    n_odd = DEPTH // 2
    D = D_MODEL
    d = DIFF_HEAD_DIM
    f32 = jnp.float32

    def nrm(k, shape, fan_in):
        return jax.random.normal(k, shape, f32) * (fan_in ** -0.5)

    def gain(k, shape):
        return 1.0 + 0.02 * jax.random.normal(k, shape, f32)

    def bias(k, shape):
        return 0.01 * jax.random.normal(k, shape, f32)

    a0 = jax.random.uniform(ks[27], (n_odd, D_RNN), f32, 0.9, 0.999)
    s0 = a0 ** (1.0 / LRU_C)
    lru_lambda = jnp.log(s0) - jnp.log1p(-s0)

    return {
        'x': jax.random.normal(ks[0], (BATCH, SEQ, D), f32),
        'p': jax.random.normal(ks[1], (DEPTH, BATCH, SEQ, D_PLE), f32),
        'ln_mix': gain(ks[2], (DEPTH, D)),
        'ln_ffn': gain(ks[3], (DEPTH, D)),
        'ln_ple': gain(ks[4], (DEPTH, D)),
        'ln_final': gain(ks[5], (D,)),
        'ple_w_proj': nrm(ks[6], (DEPTH, D_PLE, D), D_PLE),
        'ple_w_gate': nrm(ks[7], (DEPTH, D, D), D),
        'attn_wq': nrm(ks[8], (n_even, D, 2 * N_DIFF_HEADS * d), D),
        'attn_wk': nrm(ks[9], (n_even, D, 2 * N_DIFF_HEADS * d), D),
        'attn_wv': nrm(ks[10], (n_even, D, 2 * N_DIFF_HEADS * d), D),
        'attn_wo': nrm(ks[11], (n_even, 2 * N_DIFF_HEADS * d, D), 2 * N_DIFF_HEADS * d),
        'attn_lambda_q1': 0.1 * jax.random.normal(ks[12], (n_even, d), f32),
        'attn_lambda_k1': 0.1 * jax.random.normal(ks[13], (n_even, d), f32),
        'attn_lambda_q2': 0.1 * jax.random.normal(ks[14], (n_even, d), f32),
        'attn_lambda_k2': 0.1 * jax.random.normal(ks[15], (n_even, d), f32),
        'attn_subln': gain(ks[16], (n_even, 2 * d)),
        'ffn_w_gate': nrm(ks[17], (n_even, D, D_FF), D),
        'ffn_w_up': nrm(ks[18], (n_even, D, D_FF), D),
        'ffn_w_down': nrm(ks[19], (n_even, D_FF, D), D_FF),
        'lru_w_in_x': nrm(ks[20], (n_odd, D, D_RNN), D),
        'lru_w_in_y': nrm(ks[21], (n_odd, D, D_RNN), D),
        'lru_conv_w': nrm(ks[22], (n_odd, CONV_WIDTH, D_RNN), CONV_WIDTH),
        'lru_conv_b': bias(ks[23], (n_odd, D_RNN)),
        'lru_gate_a_w': nrm(ks[24], (n_odd, N_LRU_BLOCKS, LRU_BLOCK, LRU_BLOCK), LRU_BLOCK),
        'lru_gate_a_b': bias(ks[25], (n_odd, D_RNN)),
        'lru_gate_x_w': nrm(ks[26], (n_odd, N_LRU_BLOCKS, LRU_BLOCK, LRU_BLOCK), LRU_BLOCK),
        'lru_gate_x_b': bias(ks[28], (n_odd, D_RNN)),
        'lru_lambda': lru_lambda,
        'lru_w_out': nrm(ks[29], (n_odd, D_RNN, D), D_RNN),
        'moe_w_router': nrm(ks[30], (n_odd, D, N_EXPERTS), D),
        'moe_w_gate': nrm(ks[31], (n_odd, N_EXPERTS, D, D_FF_EXPERT), D),
        'moe_w_up': nrm(ks[32], (n_odd, N_EXPERTS, D, D_FF_EXPERT), D),
        'moe_w_down': nrm(ks[33], (n_odd, N_EXPERTS, D_FF_EXPERT, D), D_FF_EXPERT),
    }


def reference(x, p, ln_mix, ln_ffn, ln_ple, ln_final, ple_w_proj, ple_w_gate,
              attn_wq, attn_wk, attn_wv, attn_wo, attn_lambda_q1, attn_lambda_k1,
              attn_lambda_q2, attn_lambda_k2, attn_subln,
              ffn_w_gate, ffn_w_up, ffn_w_down,
              lru_w_in_x, lru_w_in_y, lru_conv_w, lru_conv_b, lru_gate_a_w, lru_gate_a_b,
              lru_gate_x_w, lru_gate_x_b, lru_lambda, lru_w_out,
              moe_w_router, moe_w_gate, moe_w_up, moe_w_down):
    h = x
    for i in range(DEPTH):
        j = i // 2
        hn = rmsnorm(h, ln_mix[i])
        if i % 2 == 0:
            lambda_init = 0.8 - 0.6 * math.exp(-0.3 * i)
            h = h + diff_attention(hn, attn_wq[j], attn_wk[j], attn_wv[j], attn_wo[j],
                                   attn_lambda_q1[j], attn_lambda_k1[j],
                                   attn_lambda_q2[j], attn_lambda_k2[j],
                                   attn_subln[j], lambda_init)
            h = h + swiglu(rmsnorm(h, ln_ffn[i]), ffn_w_gate[j], ffn_w_up[j], ffn_w_down[j])
        else:
            h = h + recurrent_block(hn, lru_w_in_x[j], lru_w_in_y[j], lru_conv_w[j], lru_conv_b[j],
                                    lru_gate_a_w[j], lru_gate_a_b[j], lru_gate_x_w[j],
                                    lru_gate_x_b[j], lru_lambda[j], lru_w_out[j])
            h = h + moe_swiglu(rmsnorm(h, ln_ffn[i]), moe_w_router[j], moe_w_gate[j],
                               moe_w_up[j], moe_w_down[j])
        h = h + per_layer_embedding(h, p[i], ln_ple[i], ple_w_proj[i], ple_w_gate[i])
    return rmsnorm(h, ln_final)
```

```python
import functools
import math

import jax
import jax.numpy as jnp
from jax import lax
from jax.experimental import pallas as pl
from jax.experimental.pallas import tpu as pltpu

N_DIFF_HEADS = 8
DIFF_HEAD_DIM = 64
HEAD_WIDTH = 2 * DIFF_HEAD_DIM
ROPE_THETA = 10000.0
N_LRU_BLOCKS = 8
LRU_BLOCK = 128
CONV_WIDTH = 4
LRU_C = 8.0
N_EXPERTS = 8
NORM_EPS = 1e-6
SUBLN_EPS = 1e-5

V7X_LANES = 128
V7X_SUBLANES = 8
V7X_VMEM_BYTES = 64 * 1024 * 1024

F32 = jnp.float32
BF16 = jnp.bfloat16


class _Tiles:
  def __init__(self, batch, seq):
    self.tok = min(512, seq)
    self.q = min(512, seq)
    self.expert = 512
    self.ff_chunks = 2
    n_tok = batch * seq
    assert seq % self.tok == 0 and seq % self.q == 0
    assert (2 * n_tok) % self.expert == 0
    self.n_expert_blocks = 2 * n_tok // self.expert + N_EXPERTS
    self.vmem_limit = V7X_VMEM_BYTES - 8 * 1024 * 1024


def _resident(shape):
  zeros = (0,) * len(shape)
  return pl.BlockSpec(shape, lambda *_: zeros, pipeline_mode=pl.Buffered(1))


def _rms(x, g, eps):
  ms = jnp.mean(x * x, axis=-1, keepdims=True)
  return x * lax.rsqrt(ms + eps) * g


def _dot(a, b):
  return jnp.dot(a, b, preferred_element_type=F32)


def _params(t, semantics):
  return pltpu.CompilerParams(dimension_semantics=semantics,
                              vmem_limit_bytes=t.vmem_limit)


def _qkv_kernel(h_ref, g_ref, wq_ref, wk_ref, wv_ref, cos_ref, sin_ref,
                q_ref, k_ref, v_ref):
  xn = _rms(h_ref[...], g_ref[...], NORM_EPS).astype(BF16)
  cos = cos_ref[...]
  sin = sin_ref[...]
  lane = lax.broadcasted_iota(jnp.int32, cos.shape, 1)
  low_half = (lane % DIFF_HEAD_DIM) < (DIFF_HEAD_DIM // 2)

  def rope_store(y, out_ref, scale):
    for hd in range(N_DIFF_HEADS):
      sl = slice(hd * HEAD_WIDTH, (hd + 1) * HEAD_WIDTH)
      ys = y[:, sl]
      partner = jnp.where(low_half,
                          pltpu.roll(ys, HEAD_WIDTH - DIFF_HEAD_DIM // 2, 1),
                          pltpu.roll(ys, DIFF_HEAD_DIM // 2, 1))
      out_ref[:, sl] = ((ys * cos + partner * sin) * scale).astype(out_ref.dtype)

  rope_store(_dot(xn, wq_ref[...]), q_ref, DIFF_HEAD_DIM ** -0.5)
  rope_store(_dot(xn, wk_ref[...]), k_ref, 1.0)
  v_ref[...] = _dot(xn, wv_ref[...]).astype(v_ref.dtype)


def _qkv_call(t, h, g, wq, wk, wv, cos_t, sin_t, seq):
  n, d = h.shape
  tm = t.tok
  pos_blocks = seq // tm
  tok_spec = pl.BlockSpec((tm, d), lambda i: (i, 0))
  tab_spec = pl.BlockSpec((tm, HEAD_WIDTH), lambda i: (i % pos_blocks, 0))
  out = jax.ShapeDtypeStruct((n, d), BF16)
  return pl.pallas_call(
      _qkv_kernel,
      grid=(n // tm,),
      in_specs=[tok_spec, _resident((1, d)), _resident((d, d)), _resident((d, d)),
                _resident((d, d)), tab_spec, tab_spec],
      out_specs=[tok_spec, tok_spec, tok_spec],
      out_shape=[out, out, out],
      compiler_params=_params(t, ("parallel",)),
      name="qkv_rope",
  )(h, g, wq, wk, wv, cos_t, sin_t)


def _flash_kernel(q_ref, k_ref, v_ref, lam_ref, g_ref, o_ref, m_scr, l_scr, acc_scr,
                  *, tq, lambda_init):
  i = pl.program_id(2)
  q = q_ref[...]
  lane = lax.broadcasted_iota(jnp.int32, q.shape, 1)
  zero = jnp.zeros_like(q)
  qm = jnp.concatenate([jnp.where(lane < DIFF_HEAD_DIM, q, zero),
                        jnp.where(lane >= DIFF_HEAD_DIM, q, zero)], axis=0)

  m_scr[...] = jnp.full(m_scr.shape, -jnp.inf, F32)
  l_scr[...] = jnp.zeros(l_scr.shape, F32)
  acc_scr[...] = jnp.zeros(acc_scr.shape, F32)

  def step(j, causal):
    start = pl.multiple_of(j * tq, tq)
    kb = k_ref[pl.ds(start, tq), :]
    vb = v_ref[pl.ds(start, tq), :]
    s = lax.dot_general(qm, kb, (((1,), (1,)), ((), ())), preferred_element_type=F32)
    if causal:
      row = lax.broadcasted_iota(jnp.int32, s.shape, 0) % tq
      col = lax.broadcasted_iota(jnp.int32, s.shape, 1)
      s = jnp.where(col <= row, s, -jnp.inf)
    m_prev = m_scr[...]
    m_new = jnp.maximum(m_prev, jnp.max(s, axis=1, keepdims=True))
    alpha = jnp.exp(m_prev - m_new)
    p = jnp.exp(s - jnp.concatenate([m_new] * (tq // V7X_LANES), axis=1))
    l_scr[...] = alpha * l_scr[...] + jnp.sum(p, axis=1, keepdims=True)
    acc_scr[...] = alpha * acc_scr[...] + _dot(p.astype(BF16), vb)
    m_scr[...] = m_new

  def body(j, carry):
    step(j, False)
    return carry

  lax.fori_loop(0, i, body, 0)
  step(i, True)

  lam_v = lam_ref[...]
  lam = (jnp.exp(jnp.sum(lam_v[0:1] * lam_v[1:2], axis=1, keepdims=True))
         - jnp.exp(jnp.sum(lam_v[2:3] * lam_v[3:4], axis=1, keepdims=True))
         + lambda_init)
  o_all = acc_scr[...] / l_scr[...]
  o = o_all[:tq] - lam * o_all[tq:]
  o_ref[...] = (_rms(o, g_ref[...], SUBLN_EPS) * (1.0 - lambda_init)).astype(o_ref.dtype)


def _flash_call(t, q, k, v, lam_v, subln_g, lambda_init, batch, seq):
  n, d = q.shape
  tq = t.q
  nq = seq // tq
  q_spec = pl.BlockSpec((tq, HEAD_WIDTH), lambda b, h, i: (b * nq + i, h))
  kv_spec = pl.BlockSpec((seq, HEAD_WIDTH), lambda b, h, i: (b, h))
  state = pltpu.VMEM((2 * tq, HEAD_WIDTH), F32)
  return pl.pallas_call(
      functools.partial(_flash_kernel, tq=tq, lambda_init=lambda_init),
      grid=(batch, N_DIFF_HEADS, nq),
      in_specs=[q_spec, kv_spec, kv_spec,
                pl.BlockSpec(lam_v.shape, lambda b, h, i: (0, 0)),
                pl.BlockSpec(subln_g.shape, lambda b, h, i: (0, 0))],
      out_specs=q_spec,
      out_shape=jax.ShapeDtypeStruct((n, d), BF16),
      scratch_shapes=[state, state, state],
      compiler_params=_params(t, ("parallel", "parallel", "arbitrary")),
      name="diff_flash",
  )(q, k, v, lam_v, subln_g)


def _ple(h, p, ln_g, w_gate, w_proj):
  gate = jax.nn.sigmoid(_dot(_rms(h, ln_g, NORM_EPS).astype(BF16), w_gate))
  return h + gate * _dot(p.astype(BF16), w_proj)


def _dense_tail_kernel(h_ref, o_ref, p_ref, wo_ref, ln_ffn_ref, wg_ref, wu_ref, wd_ref,
                       ln_ple_ref, wpg_ref, wpp_ref, out_ref, *, ff_chunks):
  h1 = h_ref[...] + _dot(o_ref[...], wo_ref[...])
  xn = _rms(h1, ln_ffn_ref[...], NORM_EPS).astype(BF16)
  d_ff = wg_ref.shape[1]
  cw = d_ff // ff_chunks
  y = None
  for c in range(ff_chunks):
    sl = slice(c * cw, (c + 1) * cw)
    g = _dot(xn, wg_ref[:, sl])
    u = _dot(xn, wu_ref[:, sl])
    part = _dot((g * jax.nn.sigmoid(g) * u).astype(BF16), wd_ref[sl, :])
    y = part if y is None else y + part
  h2 = h1 + y
  out_ref[...] = _ple(h2, p_ref[...], ln_ple_ref[...], wpg_ref[...], wpp_ref[...])


def _dense_tail_call(t, h, o, p, wo, ln_ffn, wg, wu, wd, ln_ple, wpg, wpp):
  n, d = h.shape
  tm = t.tok
  d_ff = wg.shape[1]
  d_ple = p.shape[1]
  assert d_ff % (t.ff_chunks * V7X_LANES) == 0
  tok_spec = pl.BlockSpec((tm, d), lambda i: (i, 0))
  return pl.pallas_call(
      functools.partial(_dense_tail_kernel, ff_chunks=t.ff_chunks),
      grid=(n // tm,),
      in_specs=[tok_spec, tok_spec, pl.BlockSpec((tm, d_ple), lambda i: (i, 0)),
                _resident((d, d)), _resident((1, d)), _resident((d, d_ff)),
                _resident((d, d_ff)), _resident((d_ff, d)), _resident((1, d)),
                _resident((d, d)), _resident((d_ple, d))],
      out_specs=tok_spec,
      out_shape=jax.ShapeDtypeStruct((n, d), F32),
      compiler_params=_params(t, ("parallel",)),
      name="dense_tail",
  )(h, o, p, wo, ln_ffn, wg, wu, wd, ln_ple, wpg, wpp)


def _rec_kernel(h_ref, ln_mix_ref, wx_ref, wy_ref, convw_ref, convb_ref, wa_ref, ba_ref,
                wi_ref, bi_ref, lam_ref, wout_ref, ln_ffn_ref, wr_ref,
                h1_ref, xn2_ref, logit_ref,
                xs_scr, a_scr, b_scr, hs_scr, carry_scr):
  tt = pl.program_id(1)
  ts = h_ref.shape[0]
  pad = V7X_SUBLANES

  @pl.when(tt == 0)
  def _():
    xs_scr[0:pad, :] = jnp.zeros((pad, xs_scr.shape[1]), F32)
    carry_scr[...] = jnp.zeros(carry_scr.shape, F32)

  x = h_ref[...]
  xn = _rms(x, ln_mix_ref[...], NORM_EPS).astype(BF16)
  yb = jax.nn.gelu(_dot(xn, wy_ref[...]), approximate=True)
  xs_scr[pad:pad + ts, :] = _dot(xn, wx_ref[...])

  full = xs_scr[...]
  cw = convw_ref[...]
  u = convb_ref[...] + cw[CONV_WIDTH - 1:CONV_WIDTH] * full[pad:]
  for back in range(1, CONV_WIDTH):
    tap = CONV_WIDTH - 1 - back
    u = u + cw[tap:tap + 1] * pltpu.roll(full, back, 0)[pad:]
  xs_scr[0:pad, :] = full[ts:ts + pad]

  ub = u.astype(BF16)
  r_parts, i_parts = [], []
  for nb in range(N_LRU_BLOCKS):
    sl = slice(nb * LRU_BLOCK, (nb + 1) * LRU_BLOCK)
    r_parts.append(_dot(ub[:, sl], wa_ref[nb]))
    i_parts.append(_dot(ub[:, sl], wi_ref[nb]))
  r = jax.nn.sigmoid(jnp.concatenate(r_parts, axis=1) + ba_ref[...])
  gate_i = jax.nn.sigmoid(jnp.concatenate(i_parts, axis=1) + bi_ref[...])
  neg_lam = -lam_ref[...]
  softplus = jnp.maximum(neg_lam, 0.0) + jnp.log1p(jnp.exp(-jnp.abs(neg_lam)))
  log_a = (-LRU_C) * r * softplus
  a = jnp.exp(log_a)
  mult = jnp.sqrt(-jnp.tanh(log_a) * (a * a + 1.0))
  a_scr[...] = a
  b_scr[...] = mult * gate_i * u

  row = lax.broadcasted_iota(jnp.int32, (V7X_SUBLANES, a_scr.shape[1]), 0)

  def scan_group(gi, carry):
    r0 = pl.multiple_of(gi * V7X_SUBLANES, V7X_SUBLANES)
    av = a_scr[pl.ds(r0, V7X_SUBLANES), :]
    bv = b_scr[pl.ds(r0, V7X_SUBLANES), :]
    for sh in (1, 2, 4):
      keep = row >= sh
      a_prev = jnp.where(keep, pltpu.roll(av, sh, 0), 1.0)
      b_prev = jnp.where(keep, pltpu.roll(bv, sh, 0), 0.0)
      bv = av * b_prev + bv
      av = av * a_prev
    hv = av * carry + bv
    hs_scr[pl.ds(r0, V7X_SUBLANES), :] = hv
    return jnp.broadcast_to(hv[V7X_SUBLANES - 1:V7X_SUBLANES, :], hv.shape)

  carry0 = jnp.broadcast_to(carry_scr[...], row.shape)
  carry = lax.fori_loop(0, ts // V7X_SUBLANES, scan_group, carry0, unroll=2)
  carry_scr[...] = carry[0:1, :]

  h1 = x + _dot((hs_scr[...] * yb).astype(BF16), wout_ref[...])
  h1_ref[...] = h1
  xn2 = _rms(h1, ln_ffn_ref[...], NORM_EPS)
  xn2_ref[...] = xn2
  logit_ref[...] = jnp.dot(xn2, wr_ref[...], preferred_element_type=F32,
                           precision=lax.Precision.HIGHEST)


def _rec_call(t, h, batch, seq, ln_mix, wx, wy, conv_w, conv_b, wa, ba, wi, bi, lam, wout,
              ln_ffn, w_router):
  n, d = h.shape
  ts = t.tok
  nt = seq // ts
  tok_spec = pl.BlockSpec((ts, d), lambda b, s: (b * nt + s, 0))
  scr = pltpu.VMEM((ts, d), F32)
  return pl.pallas_call(
      _rec_kernel,
      grid=(batch, nt),
      in_specs=[tok_spec, _resident((1, d)), _resident((d, d)), _resident((d, d)),
                _resident(conv_w.shape), _resident((1, d)), _resident(wa.shape),
                _resident((1, d)), _resident(wi.shape), _resident((1, d)),
                _resident((1, d)), _resident((d, d)), _resident((1, d)),
                _resident(w_router.shape)],
      out_specs=[tok_spec, tok_spec,
                 pl.BlockSpec((ts, V7X_LANES), lambda b, s: (b * nt + s, 0))],
      out_shape=[jax.ShapeDtypeStruct((n, d), F32), jax.ShapeDtypeStruct((n, d), F32),
                 jax.ShapeDtypeStruct((n, V7X_LANES), F32)],
      scratch_shapes=[pltpu.VMEM((ts + V7X_SUBLANES, d), F32), scr, scr, scr,
                      pltpu.VMEM((1, d), F32)],
      compiler_params=_params(t, ("parallel", "arbitrary")),
      name="recurrent_block",
  )(h, ln_mix, wx, wy, conv_w, conv_b, wa, ba, wi, bi, lam, wout, ln_ffn, w_router)


def _route_kernel(logit_ref, ids_ref, gates_ref, count_ref, cnt_scr):
  @pl.when(pl.program_id(0) == 0)
  def _():
    cnt_scr[...] = jnp.zeros(cnt_scr.shape, F32)

  lg = logit_ref[...]
  tr = lg.shape[0]
  lane = lax.broadcasted_iota(jnp.int32, lg.shape, 1)
  lg = jnp.where(lane < N_EXPERTS, lg, -jnp.inf)
  m1 = jnp.max(lg, axis=1, keepdims=True)
  e1 = jnp.min(jnp.where(lg == m1, lane, V7X_LANES), axis=1, keepdims=True)
  lg2 = jnp.where(lane == e1, -jnp.inf, lg)
  m2 = jnp.max(lg2, axis=1, keepdims=True)
  e2 = jnp.min(jnp.where(lg2 == m2, lane, V7X_LANES), axis=1, keepdims=True)
  ex = jnp.exp(m2 - m1)
  g1 = 1.0 / (1.0 + ex)
  g2 = ex / (1.0 + ex)

  chosen = jnp.where((lane == e1) | (lane == e2), 1.0, 0.0)
  r_i = lax.broadcasted_iota(jnp.int32, (tr, tr), 0)
  c_i = lax.broadcasted_iota(jnp.int32, (tr, tr), 1)
  earlier = jnp.where(c_i < r_i, 1.0, 0.0).astype(BF16)
  rank = _dot(earlier, chosen.astype(BF16)) + cnt_scr[...]
  rank1 = jnp.sum(jnp.where(lane == e1, rank, 0.0), axis=1, keepdims=True)
  rank2 = jnp.sum(jnp.where(lane == e2, rank, 0.0), axis=1, keepdims=True)
  cnt_scr[...] = cnt_scr[...] + jnp.sum(chosen, axis=0, keepdims=True)

  ids = jnp.where(lane == 0, e1, jnp.where(lane == 1, e2, 0))
  ranks = jnp.where(lane == 2, rank1, jnp.where(lane == 3, rank2, 0.0))
  ids_ref[...] = ids + ranks.astype(jnp.int32)
  gates_ref[...] = jnp.where(lane == 0, g1, jnp.where(lane == 1, g2, 0.0))
  count_ref[...] = cnt_scr[...]


def _route_call(t, logits):
  n = logits.shape[0]
  tr = t.tok
  spec = pl.BlockSpec((tr, V7X_LANES), lambda i: (i, 0))
  return pl.pallas_call(
      _route_kernel,
      grid=(n // tr,),
      in_specs=[spec],
      out_specs=[spec, spec, pl.BlockSpec((1, V7X_LANES), lambda i: (0, 0))],
      out_shape=[jax.ShapeDtypeStruct((n, V7X_LANES), jnp.int32),
                 jax.ShapeDtypeStruct((n, V7X_LANES), F32),
                 jax.ShapeDtypeStruct((1, V7X_LANES), F32)],
      scratch_shapes=[pltpu.VMEM((1, V7X_LANES), F32)],
      compiler_params=_params(t, ("arbitrary",)),
      name="route_top2",
  )(logits)


def _row_copy(src, src_row, dst, dst_row, sem):
  return pltpu.make_async_copy(src.at[pl.ds(src_row, 1), :], dst.at[pl.ds(dst_row, 1), :],
                               sem)


def _load_dest(dest_hbm, dest_smem, sem, tile, rows):
  cp = pltpu.make_async_copy(dest_hbm.at[:, pl.ds(tile * rows, rows)], dest_smem, sem)
  cp.start()
  cp.wait()


def _dispatch_kernel(dest_hbm, x_ref, zeros_hbm, xs_hbm, dest_smem, dest_sem, row_sem):
  del zeros_hbm
  rows = x_ref.shape[0]
  _load_dest(dest_hbm, dest_smem, dest_sem, pl.program_id(0), rows)

  def issue(r, carry):
    for kk in range(2):
      _row_copy(x_ref, r, xs_hbm, dest_smem[kk, r], row_sem).start()
    return carry

  lax.fori_loop(0, rows, issue, 0)
  for _ in range(2):
    pltpu.make_async_copy(x_ref, xs_hbm.at[pl.ds(0, rows), :], row_sem).wait()


def _dispatch_call(t, dest, x, n_slots):
  n, d = x.shape
  tm = t.tok
  zeros = jnp.zeros((n_slots, d), x.dtype)
  return pl.pallas_call(
      _dispatch_kernel,
      grid=(n // tm,),
      in_specs=[pl.BlockSpec(memory_space=pl.ANY),
                pl.BlockSpec((tm, d), lambda i: (i, 0)),
                pl.BlockSpec(memory_space=pl.ANY)],
      out_specs=pl.BlockSpec(memory_space=pl.ANY),
      out_shape=jax.ShapeDtypeStruct((n_slots, d), x.dtype),
      scratch_shapes=[pltpu.SMEM((2, tm), jnp.int32), pltpu.SemaphoreType.DMA,
                      pltpu.SemaphoreType.DMA],
      input_output_aliases={2: 0},
      compiler_params=_params(t, ("arbitrary",)),
      name="moe_dispatch",
  )(dest, x, zeros)


def _expert_kernel(be_ref, used_ref, xs_ref, wg_ref, wu_ref, wd_ref, ys_ref, *, ff_chunks):
  del be_ref
  i = pl.program_id(0)

  @pl.when(i < used_ref[0])
  def _():
    x = xs_ref[...].astype(BF16)
    d_ff = wg_ref.shape[1]
    cw = d_ff // ff_chunks
    y = None
    for c in range(ff_chunks):
      sl = slice(c * cw, (c + 1) * cw)
      g = _dot(x, wg_ref[:, sl])
      u = _dot(x, wu_ref[:, sl])
      part = _dot((g * jax.nn.sigmoid(g) * u).astype(BF16), wd_ref[sl, :])
      y = part if y is None else y + part
    ys_ref[...] = y

  @pl.when(i >= used_ref[0])
  def _():
    ys_ref[...] = jnp.zeros(ys_ref.shape, ys_ref.dtype)


def _expert_call(t, block_expert, n_used, xs, wg, wu, wd):
  n_slots, d = xs.shape
  tm = t.expert
  d_ff = wg.shape[2]
  assert d_ff % (t.ff_chunks * V7X_LANES) == 0
  row_spec = pl.BlockSpec((tm, d), lambda i, be, used: (i, 0))
  once = pl.Buffered(1)
  grid_spec = pltpu.PrefetchScalarGridSpec(
      num_scalar_prefetch=2,
      grid=(n_slots // tm,),
      in_specs=[row_spec,
                pl.BlockSpec((None, d, d_ff), lambda i, be, used: (be[i], 0, 0),
                             pipeline_mode=once),
                pl.BlockSpec((None, d, d_ff), lambda i, be, used: (be[i], 0, 0),
                             pipeline_mode=once),
                pl.BlockSpec((None, d_ff, d), lambda i, be, used: (be[i], 0, 0),
                             pipeline_mode=once)],
      out_specs=row_spec,
  )
  return pl.pallas_call(
      functools.partial(_expert_kernel, ff_chunks=t.ff_chunks),
      grid_spec=grid_spec,
      out_shape=jax.ShapeDtypeStruct((n_slots, d), F32),
      compiler_params=_params(t, ("arbitrary",)),
      name="moe_experts",
  )(block_expert, n_used, xs, wg, wu, wd)


def _combine_kernel(dest_hbm, ys_hbm, h1_ref, gates_ref, p_ref, ln_ple_ref, wpg_ref,
                    wpp_ref, ln_final_ref, out_ref, dest_smem, y1_scr, y2_scr, dest_sem,
                    row_sem, *, final_norm):
  rows = h1_ref.shape[0]
  _load_dest(dest_hbm, dest_smem, dest_sem, pl.program_id(0), rows)

  def issue(r, carry):
    _row_copy(ys_hbm, dest_smem[0, r], y1_scr, r, row_sem).start()
    _row_copy(ys_hbm, dest_smem[1, r], y2_scr, r, row_sem).start()
    return carry

  lax.fori_loop(0, rows, issue, 0)
  for scr in (y1_scr, y2_scr):
    pltpu.make_async_copy(ys_hbm.at[pl.ds(0, rows), :], scr, row_sem).wait()

  gates = gates_ref[...]
  h2 = h1_ref[...] + (gates[:, 0:1] * y1_scr[...] + gates[:, 1:2] * y2_scr[...])
  h3 = _ple(h2, p_ref[...], ln_ple_ref[...], wpg_ref[...], wpp_ref[...])
  if final_norm:
    h3 = _rms(h3, ln_final_ref[...], NORM_EPS)
  out_ref[...] = h3


def _combine_call(t, dest, ys, h1, gates, p, ln_ple, wpg, wpp, ln_final, final_norm):
  n, d = h1.shape
  tm = t.tok
  d_ple = p.shape[1]
  tok_spec = pl.BlockSpec((tm, d), lambda i: (i, 0))
  row_scr = pltpu.VMEM((tm, d), F32)
  return pl.pallas_call(
      functools.partial(_combine_kernel, final_norm=final_norm),
      grid=(n // tm,),
      in_specs=[pl.BlockSpec(memory_space=pl.ANY), pl.BlockSpec(memory_space=pl.ANY),
                tok_spec, pl.BlockSpec((tm, V7X_LANES), lambda i: (i, 0)),
                pl.BlockSpec((tm, d_ple), lambda i: (i, 0)),
                _resident((1, d)), _resident((d, d)), _resident((d_ple, d)),
                _resident((1, d))],
      out_specs=tok_spec,
      out_shape=jax.ShapeDtypeStruct((n, d), F32),
      scratch_shapes=[pltpu.SMEM((2, tm), jnp.int32), row_scr, row_scr,
                      pltpu.SemaphoreType.DMA, pltpu.SemaphoreType.DMA],
      compiler_params=_params(t, ("arbitrary",)),
      name="moe_combine_ple",
  )(dest, ys, h1, gates, p, ln_ple, wpg, wpp, ln_final)


def _rope_tables(seq):
  inv = 1.0 / (ROPE_THETA ** (jnp.arange(0, DIFF_HEAD_DIM, 2, dtype=F32) / DIFF_HEAD_DIM))
  ang = jnp.arange(seq, dtype=F32)[:, None] * inv[None, :]
  cos, sin = jnp.cos(ang), jnp.sin(ang)
  reps = HEAD_WIDTH // DIFF_HEAD_DIM
  cos_t = jnp.tile(jnp.concatenate([cos, cos], axis=1), (1, reps))
  sin_t = jnp.tile(jnp.concatenate([-sin, sin], axis=1), (1, reps))
  return cos_t, sin_t


def _slot_layout(t, ids, counts_f):
  counts = counts_f[0, :N_EXPERTS].astype(jnp.int32)
  padded = (counts + t.expert - 1) // t.expert * t.expert
  seg_end = jnp.cumsum(padded)
  seg_start = seg_end - padded
  e1, e2, r1, r2 = ids[:, 0], ids[:, 1], ids[:, 2], ids[:, 3]
  onehot = lambda e: e[:, None] == jnp.arange(N_EXPERTS, dtype=jnp.int32)[None, :]
  start_of = lambda e: jnp.sum(jnp.where(onehot(e), seg_start[None, :], 0), axis=1)
  dest = jnp.stack([start_of(e1) + r1, start_of(e2) + r2]).astype(jnp.int32)
  block_start = jnp.arange(t.n_expert_blocks, dtype=jnp.int32) * t.expert
  block_expert = jnp.minimum(
      jnp.searchsorted(seg_end, block_start, side='right'), N_EXPERTS - 1).astype(jnp.int32)
  n_used = (seg_end[-1:] // t.expert).astype(jnp.int32)
  return dest, block_expert, n_used


def kernel(x, p, ln_mix, ln_ffn, ln_ple, ln_final, ple_w_proj, ple_w_gate, attn_wq, attn_wk, attn_wv, attn_wo, attn_lambda_q1, attn_lambda_k1, attn_lambda_q2, attn_lambda_k2, attn_subln, ffn_w_gate, ffn_w_up, ffn_w_down, lru_w_in_x, lru_w_in_y, lru_conv_w, lru_conv_b, lru_gate_a_w, lru_gate_a_b, lru_gate_x_w, lru_gate_x_b, lru_lambda, lru_w_out, moe_w_router, moe_w_gate, moe_w_up, moe_w_down):
  batch, seq, d = x.shape
  depth = p.shape[0]
  n = batch * seq
  t = _Tiles(batch, seq)
  row = lambda v: v.reshape(1, -1)
  bf = lambda w: w.astype(BF16)
  cos_t, sin_t = _rope_tables(seq)
  n_slots = t.n_expert_blocks * t.expert

  h = x.reshape(n, d)
  for i in range(depth):
    j = i // 2
    p_i = p[i].reshape(n, -1)
    if i % 2 == 0:
      lambda_init = 0.8 - 0.6 * math.exp(-0.3 * i)
      q, k, v = _qkv_call(t, h, row(ln_mix[i]), bf(attn_wq[j]), bf(attn_wk[j]),
                          bf(attn_wv[j]), cos_t, sin_t, seq)
      lam_v = jnp.stack([attn_lambda_q1[j], attn_lambda_k1[j],
                         attn_lambda_q2[j], attn_lambda_k2[j]])
      o = _flash_call(t, q, k, v, lam_v, row(attn_subln[j]), lambda_init, batch, seq)
      h = _dense_tail_call(t, h, o, p_i, bf(attn_wo[j]), row(ln_ffn[i]),
                           bf(ffn_w_gate[j]), bf(ffn_w_up[j]), bf(ffn_w_down[j]),
                           row(ln_ple[i]), bf(ple_w_gate[i]), bf(ple_w_proj[i]))
    else:
      w_router = jnp.pad(moe_w_router[j], ((0, 0), (0, V7X_LANES - N_EXPERTS)))
      h1, xn2, logits = _rec_call(
          t, h, batch, seq, row(ln_mix[i]), bf(lru_w_in_x[j]), bf(lru_w_in_y[j]),
          lru_conv_w[j], row(lru_conv_b[j]), bf(lru_gate_a_w[j]), row(lru_gate_a_b[j]),
          bf(lru_gate_x_w[j]), row(lru_gate_x_b[j]), row(lru_lambda[j]),
          bf(lru_w_out[j]), row(ln_ffn[i]), w_router)
      ids, gates, counts = _route_call(t, logits)
      dest, block_expert, n_used = _slot_layout(t, ids, counts)
      xs = _dispatch_call(t, dest, xn2, n_slots)
      ys = _expert_call(t, block_expert, n_used, xs, bf(moe_w_gate[j]), bf(moe_w_up[j]),
                        bf(moe_w_down[j]))
      h = _combine_call(t, dest, ys, h1, gates, p_i, row(ln_ple[i]), bf(ple_w_gate[i]),
                        bf(ple_w_proj[i]), row(ln_final), final_norm=(i == depth - 1))
  if depth % 2 == 1:
    raise NotImplementedError("final norm is fused into the last (odd) layer")
  return h.reshape(batch, seq, d)
```

```python
import functools
import math

import jax
import jax.numpy as jnp
from jax import lax
from jax.experimental import pallas as pl
from jax.experimental.pallas import tpu as pltpu

N_DIFF_HEADS = 8
DIFF_HEAD_DIM = 64
HEAD_WIDTH = 2 * DIFF_HEAD_DIM
ROPE_THETA = 10000.0
N_LRU_BLOCKS = 8
LRU_BLOCK = 128
CONV_WIDTH = 4
LRU_C = 8.0
N_EXPERTS = 8
NORM_EPS = 1e-6
SUBLN_EPS = 1e-5

V7X_LANES = 128
V7X_SUBLANES = 8
V7X_VMEM_BYTES = 64 * 1024 * 1024

F32 = jnp.float32
BF16 = jnp.bfloat16


class _Tiles:
  def __init__(self, batch, seq):
    self.tok = min(512, seq)
    self.q = min(512, seq)
    self.expert = 512
    self.ff_chunks = 2
    self.heads_per_step = 2
    self.attn_chunk = 256
    n_tok = batch * seq
    assert seq % self.tok == 0 and seq % self.q == 0
    run_pad = N_EXPERTS * (V7X_SUBLANES - 1)
    self.pair_rows = 2 * self.tok + 64
    assert self.pair_rows >= 2 * self.tok + run_pad and self.expert <= self.pair_rows
    max_slots = 2 * n_tok + (n_tok // self.tok) * run_pad
    self.n_expert_blocks = -(-max_slots // self.expert) + N_EXPERTS
    self.vmem_limit = V7X_VMEM_BYTES - 8 * 1024 * 1024


def _resident(shape):
  zeros = (0,) * len(shape)
  return pl.BlockSpec(shape, lambda *_: zeros, pipeline_mode=pl.Buffered(1))


def _rms(x, g, eps):
  ms = jnp.mean(x * x, axis=-1, keepdims=True)
  return x * lax.rsqrt(ms + eps) * g


def _dot(a, b):
  return jnp.dot(a, b, preferred_element_type=F32)


def _params(t, semantics):
  return pltpu.CompilerParams(dimension_semantics=semantics,
                              vmem_limit_bytes=t.vmem_limit)


def _qkv_kernel(h_ref, g_ref, wq_ref, wk_ref, wv_ref, cos_ref, sin_ref,
                q_ref, k_ref, v_ref):
  xn = _rms(h_ref[...], g_ref[...], NORM_EPS).astype(BF16)
  cos = cos_ref[...]
  sin = sin_ref[...]
  lane = lax.broadcasted_iota(jnp.int32, cos.shape, 1)
  low_half = (lane % DIFF_HEAD_DIM) < (DIFF_HEAD_DIM // 2)

  def rope_store(y, out_ref, scale):
    for hd in range(N_DIFF_HEADS):
      sl = slice(hd * HEAD_WIDTH, (hd + 1) * HEAD_WIDTH)
      ys = y[:, sl]
      partner = jnp.where(low_half,
                          pltpu.roll(ys, HEAD_WIDTH - DIFF_HEAD_DIM // 2, 1),
                          pltpu.roll(ys, DIFF_HEAD_DIM // 2, 1))
      out_ref[:, sl] = ((ys * cos + partner * sin) * scale).astype(out_ref.dtype)

  rope_store(_dot(xn, wq_ref[...]), q_ref, DIFF_HEAD_DIM ** -0.5 * math.log2(math.e))
  rope_store(_dot(xn, wk_ref[...]), k_ref, 1.0)
  v_ref[...] = _dot(xn, wv_ref[...]).astype(v_ref.dtype)


def _qkv_call(t, h, g, wq, wk, wv, cos_t, sin_t, seq):
  n, d = h.shape
  tm = t.tok
  pos_blocks = seq // tm
  tok_spec = pl.BlockSpec((tm, d), lambda i: (i, 0))
  tab_spec = pl.BlockSpec((tm, HEAD_WIDTH), lambda i: (i % pos_blocks, 0))
  out = jax.ShapeDtypeStruct((n, d), BF16)
  return pl.pallas_call(
      _qkv_kernel,
      grid=(n // tm,),
      in_specs=[tok_spec, _resident((1, d)), _resident((d, d)), _resident((d, d)),
                _resident((d, d)), tab_spec, tab_spec],
      out_specs=[tok_spec, tok_spec, tok_spec],
      out_shape=[out, out, out],
      compiler_params=_params(t, ("parallel",)),
      name="qkv_rope",
  )(h, g, wq, wk, wv, cos_t, sin_t)


def _flash_kernel(q_ref, k_ref, v_ref, lam_ref, g_ref, o_ref, qm_scr, m_scr, acc_scr,
                  s_scr, *, tq, chunk, heads, lambda_init):
  i = pl.program_id(2)
  lane = lax.broadcasted_iota(jnp.int32, (tq, HEAD_WIDTH), 1)
  for hd in range(heads):
    q = q_ref[:, hd * HEAD_WIDTH:(hd + 1) * HEAD_WIDTH]
    zero = jnp.zeros_like(q)
    qm_scr[hd, 0:tq, :] = jnp.where(lane < DIFF_HEAD_DIM, q, zero)
    qm_scr[hd, tq:2 * tq, :] = jnp.where(lane >= DIFF_HEAD_DIM, q, zero)
  m_scr[...] = jnp.full(m_scr.shape, -jnp.inf, F32)
  acc_scr[...] = jnp.zeros(acc_scr.shape, F32)
  ones = jnp.ones((tq, HEAD_WIDTH), BF16)

  n_chunks = 2 * tq // chunk

  def pipeline_step(j, *, consume, produce, causal=False):
    v_start = pl.multiple_of(j * tq, tq)
    k_start = pl.multiple_of((j + 1) * tq, tq) if consume else v_start
    for hd in range(heads):
      head = slice(hd * HEAD_WIDTH, (hd + 1) * HEAD_WIDTH)
      if consume:
        v1 = jnp.concatenate([v_ref[pl.ds(v_start, tq), head], ones], axis=1)
      if produce:
        kb = k_ref[pl.ds(k_start, tq), head]
      for c in range(n_chunks):
        rows = slice(c * chunk, (c + 1) * chunk)
        if consume:
          s = s_scr[hd, rows, :]
          m_prev = m_scr[hd, rows, :]
          m_new = jnp.maximum(m_prev, jnp.max(s, axis=1, keepdims=True))
          alpha = jnp.exp2(m_prev - m_new)
          p = jnp.exp2(s - jnp.concatenate([m_new] * (tq // V7X_LANES), axis=1))
          acc_scr[hd, rows, :] = (
              jnp.concatenate([alpha, alpha], axis=1) * acc_scr[hd, rows, :]
              + _dot(p.astype(BF16), v1))
          m_scr[hd, rows, :] = m_new
        if produce:
          s_new = lax.dot_general(qm_scr[hd, rows, :], kb, (((1,), (1,)), ((), ())),
                                  preferred_element_type=F32)
          if causal:
            row = (lax.broadcasted_iota(jnp.int32, s_new.shape, 0) + c * chunk) % tq
            col = lax.broadcasted_iota(jnp.int32, s_new.shape, 1)
            s_new = jnp.where(col <= row, s_new, -jnp.inf)
          s_scr[hd, rows, :] = s_new

  @pl.when(i == 0)
  def _():
    pipeline_step(0, consume=False, produce=True, causal=True)

  @pl.when(i > 0)
  def _():
    pipeline_step(0, consume=False, produce=True)

  def body(j, carry):
    pipeline_step(j, consume=True, produce=True)
    return carry

  lax.fori_loop(0, i - 1, body, 0)

  @pl.when(i > 0)
  def _():
    pipeline_step(i - 1, consume=True, produce=True, causal=True)

  pipeline_step(i, consume=True, produce=False)

  lam_v = lam_ref[...]
  lam = (jnp.exp(jnp.sum(lam_v[0:1] * lam_v[1:2], axis=1, keepdims=True))
         - jnp.exp(jnp.sum(lam_v[2:3] * lam_v[3:4], axis=1, keepdims=True))
         + lambda_init)
  for hd in range(heads):
    acc = acc_scr[hd]
    o_all = acc[:, :HEAD_WIDTH] / acc[:, HEAD_WIDTH:]
    o = o_all[:tq] - lam * o_all[tq:]
    o_ref[:, hd * HEAD_WIDTH:(hd + 1) * HEAD_WIDTH] = (
        _rms(o, g_ref[...], SUBLN_EPS) * (1.0 - lambda_init)).astype(o_ref.dtype)


def _flash_call(t, q, k, v, lam_v, subln_g, lambda_init, batch, seq):
  n, d = q.shape
  tq = t.q
  nq = seq // tq
  heads = t.heads_per_step
  width = heads * HEAD_WIDTH
  q_spec = pl.BlockSpec((tq, width), lambda b, h, i: (b * nq + i, h))
  kv_spec = pl.BlockSpec((seq, width), lambda b, h, i: (b, h))
  return pl.pallas_call(
      functools.partial(_flash_kernel, tq=tq, chunk=min(t.attn_chunk, tq), heads=heads,
                        lambda_init=lambda_init),
      grid=(batch, N_DIFF_HEADS // heads, nq),
      in_specs=[q_spec, kv_spec, kv_spec,
                pl.BlockSpec(lam_v.shape, lambda b, h, i: (0, 0)),
                pl.BlockSpec(subln_g.shape, lambda b, h, i: (0, 0))],
      out_specs=q_spec,
      out_shape=jax.ShapeDtypeStruct((n, d), BF16),
      scratch_shapes=[pltpu.VMEM((heads, 2 * tq, HEAD_WIDTH), BF16),
                      pltpu.VMEM((heads, 2 * tq, HEAD_WIDTH), F32),
                      pltpu.VMEM((heads, 2 * tq, 2 * HEAD_WIDTH), F32),
                      pltpu.VMEM((heads, 2 * tq, tq), F32)],
      compiler_params=_params(t, ("parallel", "parallel", "arbitrary")),
      name="diff_flash",
  )(q, k, v, lam_v, subln_g)


def _ple(h, p, ln_g, w_gate, w_proj):
  gate = jax.nn.sigmoid(_dot(_rms(h, ln_g, NORM_EPS).astype(BF16), w_gate))
  return h + gate * _dot(p.astype(BF16), w_proj)


def _dense_tail_kernel(h_ref, o_ref, p_ref, wo_ref, ln_ffn_ref, wg_ref, wu_ref, wd_ref,
                       ln_ple_ref, wpg_ref, wpp_ref, out_ref, *, ff_chunks):
  h1 = h_ref[...] + _dot(o_ref[...], wo_ref[...])
  xn = _rms(h1, ln_ffn_ref[...], NORM_EPS).astype(BF16)
  d_ff = wg_ref.shape[1]
  cw = d_ff // ff_chunks
  y = None
  for c in range(ff_chunks):
    sl = slice(c * cw, (c + 1) * cw)
    g = _dot(xn, wg_ref[:, sl])
    u = _dot(xn, wu_ref[:, sl])
    part = _dot((g * jax.nn.sigmoid(g) * u).astype(BF16), wd_ref[sl, :])
    y = part if y is None else y + part
  h2 = h1 + y
  out_ref[...] = _ple(h2, p_ref[...], ln_ple_ref[...], wpg_ref[...], wpp_ref[...])


def _dense_tail_call(t, h, o, p, wo, ln_ffn, wg, wu, wd, ln_ple, wpg, wpp):
  n, d = h.shape
  tm = t.tok
  d_ff = wg.shape[1]
  d_ple = p.shape[1]
  assert d_ff % (t.ff_chunks * V7X_LANES) == 0
  tok_spec = pl.BlockSpec((tm, d), lambda i: (i, 0))
  return pl.pallas_call(
      functools.partial(_dense_tail_kernel, ff_chunks=t.ff_chunks),
      grid=(n // tm,),
      in_specs=[tok_spec, tok_spec, pl.BlockSpec((tm, d_ple), lambda i: (i, 0)),
                _resident((d, d)), _resident((1, d)), _resident((d, d_ff)),
                _resident((d, d_ff)), _resident((d_ff, d)), _resident((1, d)),
                _resident((d, d)), _resident((d_ple, d))],
      out_specs=tok_spec,
      out_shape=jax.ShapeDtypeStruct((n, d), F32),
      compiler_params=_params(t, ("parallel",)),
      name="dense_tail",
  )(h, o, p, wo, ln_ffn, wg, wu, wd, ln_ple, wpg, wpp)


def _rec_kernel(h_ref, ln_mix_ref, wx_ref, wy_ref, convw_ref, convb_ref, wgate_ref, ba_ref,
                bi_ref, lam_ref, wout_ref, ln_ffn_ref, wr_ref,
                h1_ref, xn2_ref, logit_ref,
                xs_scr, a_scr, b_scr, hs_scr, carry_scr):
  tt = pl.program_id(1)
  ts = h_ref.shape[0]
  pad = V7X_SUBLANES

  @pl.when(tt == 0)
  def _():
    xs_scr[0:pad, :] = jnp.zeros((pad, xs_scr.shape[1]), F32)
    carry_scr[...] = jnp.zeros(carry_scr.shape, F32)

  x = h_ref[...]
  xn = _rms(x, ln_mix_ref[...], NORM_EPS).astype(BF16)
  yv = _dot(xn, wy_ref[...])
  yb = yv * jax.nn.sigmoid(
      yv * (2.0 * math.sqrt(2.0 / math.pi) * (1.0 + 0.044715 * (yv * yv))))
  xs_scr[pad:pad + ts, :] = _dot(xn, wx_ref[...])

  full = xs_scr[...]
  cw = convw_ref[...]
  u = convb_ref[...] + cw[CONV_WIDTH - 1:CONV_WIDTH] * full[pad:]
  for back in range(1, CONV_WIDTH):
    tap = CONV_WIDTH - 1 - back
    u = u + cw[tap:tap + 1] * pltpu.roll(full, back, 0)[pad:]
  xs_scr[0:pad, :] = full[ts:ts + pad]

  ub = u.astype(BF16)
  gates = [_dot(ub[:, nb * LRU_BLOCK:(nb + 1) * LRU_BLOCK], wgate_ref[nb])
           for nb in range(N_LRU_BLOCKS)]
  r = jax.nn.sigmoid(
      jnp.concatenate([g[:, :LRU_BLOCK] for g in gates], axis=1) + ba_ref[...])
  gate_i = jax.nn.sigmoid(
      jnp.concatenate([g[:, LRU_BLOCK:] for g in gates], axis=1) + bi_ref[...])
  neg_lam = -lam_ref[...]
  softplus = jnp.maximum(neg_lam, 0.0) + jnp.log1p(jnp.exp(-jnp.abs(neg_lam)))
  log_a = (-LRU_C) * r * softplus
  a = jnp.exp(log_a)
  one_m_a2 = -jnp.tanh(log_a) * (a * a + 1.0)
  mult = jnp.where(one_m_a2 > 0.0, one_m_a2 * lax.rsqrt(one_m_a2), 0.0)
  a_scr[...] = a
  b_scr[...] = mult * gate_i * u

  row = lax.broadcasted_iota(jnp.int32, (V7X_SUBLANES, a_scr.shape[1]), 0)

  def scan_group(gi, carry):
    r0 = pl.multiple_of(gi * V7X_SUBLANES, V7X_SUBLANES)
    av = a_scr[pl.ds(r0, V7X_SUBLANES), :]
    bv = b_scr[pl.ds(r0, V7X_SUBLANES), :]
    for sh in (1, 2, 4):
      keep = row >= sh
      a_prev = jnp.where(keep, pltpu.roll(av, sh, 0), 1.0)
      b_prev = jnp.where(keep, pltpu.roll(bv, sh, 0), 0.0)
      bv = av * b_prev + bv
      av = av * a_prev
    hv = av * carry + bv
    hs_scr[pl.ds(r0, V7X_SUBLANES), :] = hv
    return jnp.broadcast_to(hv[V7X_SUBLANES - 1:V7X_SUBLANES, :], hv.shape)

  carry0 = jnp.broadcast_to(carry_scr[...], row.shape)
  carry = lax.fori_loop(0, ts // V7X_SUBLANES, scan_group, carry0, unroll=2)
  carry_scr[...] = carry[0:1, :]

  h1 = x + _dot((hs_scr[...] * yb).astype(BF16), wout_ref[...])
  h1_ref[...] = h1
  xn2 = _rms(h1, ln_ffn_ref[...], NORM_EPS)
  x_hi = xn2.astype(BF16)
  xn2_ref[...] = x_hi
  x_lo = (xn2 - x_hi.astype(F32)).astype(BF16)
  parts = _dot(x_hi, wr_ref[...]) + _dot(x_lo, wr_ref[...])
  logit_ref[...] = parts[:, :V7X_LANES] + parts[:, V7X_LANES:]


def _rec_call(t, h, batch, seq, ln_mix, wx, wy, conv_w, conv_b, w_gates, ba, bi, lam, wout,
              ln_ffn, w_router):
  n, d = h.shape
  ts = t.tok
  nt = seq // ts
  tok_spec = pl.BlockSpec((ts, d), lambda b, s: (b * nt + s, 0))
  scr = pltpu.VMEM((ts, d), F32)
  return pl.pallas_call(
      _rec_kernel,
      grid=(batch, nt),
      in_specs=[tok_spec, _resident((1, d)), _resident((d, d)), _resident((d, d)),
                _resident(conv_w.shape), _resident((1, d)), _resident(w_gates.shape),
                _resident((1, d)), _resident((1, d)),
                _resident((1, d)), _resident((d, d)), _resident((1, d)),
                _resident(w_router.shape)],
      out_specs=[tok_spec, tok_spec,
                 pl.BlockSpec((ts, V7X_LANES), lambda b, s: (b * nt + s, 0))],
      out_shape=[jax.ShapeDtypeStruct((n, d), F32), jax.ShapeDtypeStruct((n, d), BF16),
                 jax.ShapeDtypeStruct((n, V7X_LANES), F32)],
      scratch_shapes=[pltpu.VMEM((ts + V7X_SUBLANES, d), F32), scr, scr, scr,
                      pltpu.VMEM((1, d), F32)],
      compiler_params=_params(t, ("parallel", "arbitrary")),
      name="recurrent_block",
  )(h, ln_mix, wx, wy, conv_w, conv_b, w_gates, ba, bi, lam, wout, ln_ffn, w_router)


def _route_kernel(logit_ref, ids_ref, gates_ref, pos_t_ref, cnt_ref):
  lg = logit_ref[...]
  tr = lg.shape[0]
  lane = lax.broadcasted_iota(jnp.int32, lg.shape, 1)
  lg = jnp.where(lane < N_EXPERTS, lg, -jnp.inf)
  m1 = jnp.max(lg, axis=1, keepdims=True)
  e1 = jnp.min(jnp.where(lg == m1, lane, V7X_LANES), axis=1, keepdims=True)
  lg2 = jnp.where(lane == e1, -jnp.inf, lg)
  m2 = jnp.max(lg2, axis=1, keepdims=True)
  e2 = jnp.min(jnp.where(lg2 == m2, lane, V7X_LANES), axis=1, keepdims=True)
  ex = jnp.exp(m2 - m1)
  g1 = 1.0 / (1.0 + ex)
  g2 = ex / (1.0 + ex)

  chosen = jnp.where((lane == e1) | (lane == e2), 1.0, 0.0)
  r_i = lax.broadcasted_iota(jnp.int32, (tr, tr), 0)
  c_i = lax.broadcasted_iota(jnp.int32, (tr, tr), 1)
  earlier = jnp.where(c_i < r_i, 1.0, 0.0).astype(BF16)
  rank = _dot(earlier, chosen.astype(BF16))
  cnt = jnp.sum(chosen, axis=0, keepdims=True)
  run = jnp.ceil(cnt * (1.0 / V7X_SUBLANES)) * V7X_SUBLANES

  def position(e):
    return jnp.sum(jnp.where(lane < e, run, jnp.where(lane == e, rank, 0.0)),
                   axis=1, keepdims=True).astype(jnp.int32)

  pos1, pos2 = position(e1), position(e2)
  ids_ref[...] = jnp.where(lane == 0, e1, jnp.where(lane == 1, e2, jnp.where(
      lane == 2, pos1, jnp.where(lane == 3, pos2, 0))))
  gates_ref[...] = jnp.where(lane == 0, g1, jnp.where(lane == 1, g2, 0.0))
  pos_cols = jnp.where(lane == 0, pos1, jnp.where(lane == 1, pos2, 0))
  pos_t_ref[0] = jnp.transpose(pos_cols)[0:V7X_SUBLANES, :]
  cnt_ref[0] = jnp.broadcast_to(cnt, cnt_ref.shape[1:])


def _route_call(t, logits):
  n = logits.shape[0]
  tr = t.tok
  spec = pl.BlockSpec((tr, V7X_LANES), lambda i: (i, 0))
  return pl.pallas_call(
      _route_kernel,
      grid=(n // tr,),
      in_specs=[spec],
      out_specs=[spec, spec,
                 pl.BlockSpec((1, V7X_SUBLANES, tr), lambda i: (i, 0, 0)),
                 pl.BlockSpec((1, V7X_SUBLANES, V7X_LANES), lambda i: (i, 0, 0))],
      out_shape=[jax.ShapeDtypeStruct((n, V7X_LANES), jnp.int32),
                 jax.ShapeDtypeStruct((n, V7X_LANES), F32),
                 jax.ShapeDtypeStruct((n // tr, V7X_SUBLANES, tr), jnp.int32),
                 jax.ShapeDtypeStruct((n // tr, V7X_SUBLANES, V7X_LANES), F32)],
      compiler_params=_params(t, ("parallel",)),
      name="route_top2",
  )(logits)


def _run_copies(run_ref, slot_ref, tile, make_copy, max_rows):
  top_bit = max_rows.bit_length() - 1
  low_bit = V7X_SUBLANES.bit_length() - 1
  assert max_rows == 1 << top_bit
  tile_row = 0
  for e in range(N_EXPERTS):
    run = run_ref[tile * N_EXPERTS + e]
    slot_row = slot_ref[tile * N_EXPERTS + e]
    for k in range(top_bit, low_bit - 1, -1):
      done = (run >> (k + 1)) << (k + 1)

      @pl.when(((run >> k) & 1) == 1)
      def _():
        make_copy(pl.multiple_of(tile_row + done, V7X_SUBLANES),
                  pl.multiple_of(slot_row + done, V7X_SUBLANES), 1 << k).start()
    tile_row = tile_row + run


def _wait_rows(hbm, vmem_buf, rows, sem):
  rows = pl.multiple_of(rows, V7X_SUBLANES)
  pltpu.make_async_copy(hbm.at[pl.ds(0, rows), :], vmem_buf.at[pl.ds(0, rows), :], sem).wait()


def _dispatch_kernel(run_ref, slot_ref, total_ref, zero_ref, x_ref, pos_t_ref, xs_hbm,
                     y_scr, sems, *, n_tiles, expert_rows):
  i = pl.program_id(0)
  buf = i % 2

  @pl.when(i == 0)
  def _():
    y_scr[1, 0:expert_rows, :] = jnp.zeros((expert_rows, y_scr.shape[2]), y_scr.dtype)

    def zero_copy(e):
      first = pl.multiple_of(zero_ref[e], expert_rows)
      return pltpu.make_async_copy(y_scr.at[1, pl.ds(0, expert_rows), :],
                                   xs_hbm.at[pl.ds(first, expert_rows), :], sems.at[1])
    for e in range(N_EXPERTS):
      @pl.when(zero_ref[e] >= 0)
      def _():
        zero_copy(e).start()
    for e in range(N_EXPERTS):
      @pl.when(zero_ref[e] >= 0)
      def _():
        zero_copy(e).wait()

    def zero_block(b, carry):
      first = pl.multiple_of(b * expert_rows, expert_rows)
      cp = pltpu.make_async_copy(y_scr.at[1, pl.ds(0, expert_rows), :],
                                 xs_hbm.at[pl.ds(first, expert_rows), :], sems.at[1])
      cp.start()
      cp.wait()
      return carry

    lax.fori_loop(zero_ref[N_EXPERTS], xs_hbm.shape[0] // expert_rows, zero_block, 0)

  pos = pos_t_ref[0]
  order = lax.broadcasted_iota(jnp.int32, (y_scr.shape[1], x_ref.shape[0]), 0)
  select = jnp.where((order == pos[0:1, :]) | (order == pos[1:2, :]), 1.0, 0.0)
  y_scr[buf] = _dot(select.astype(BF16), x_ref[...])

  def make_copy(tile_row, slot_row, rows):
    return pltpu.make_async_copy(y_scr.at[buf, pl.ds(tile_row, rows), :],
                                 xs_hbm.at[pl.ds(slot_row, rows), :], sems.at[buf])

  _run_copies(run_ref, slot_ref, i, make_copy, x_ref.shape[0])

  @pl.when(i > 0)
  def _():
    _wait_rows(xs_hbm, y_scr.at[1 - buf], total_ref[i - 1], sems.at[1 - buf])

  @pl.when(i == n_tiles - 1)
  def _():
    _wait_rows(xs_hbm, y_scr.at[buf], total_ref[i], sems.at[buf])


def _dispatch_call(t, tile_run, tile_slot, tile_total, zero_rows, x, pos_t, n_slots):
  n, d = x.shape
  tm = t.tok
  n_tiles = n // tm
  grid_spec = pltpu.PrefetchScalarGridSpec(
      num_scalar_prefetch=4,
      grid=(n_tiles,),
      in_specs=[pl.BlockSpec((tm, d), lambda i, *_: (i, 0)),
                pl.BlockSpec((1, V7X_SUBLANES, tm), lambda i, *_: (i, 0, 0))],
      out_specs=pl.BlockSpec(memory_space=pl.ANY),
      scratch_shapes=[pltpu.VMEM((2, t.pair_rows, d), F32), pltpu.SemaphoreType.DMA((2,))],
  )
  return pl.pallas_call(
      functools.partial(_dispatch_kernel, n_tiles=n_tiles, expert_rows=t.expert),
      grid_spec=grid_spec,
      out_shape=jax.ShapeDtypeStruct((n_slots, d), F32),
      compiler_params=_params(t, ("arbitrary",)),
      name="moe_dispatch",
  )(tile_run, tile_slot, tile_total, zero_rows, x, pos_t)


def _expert_kernel(be_ref, used_ref, xs_ref, wg_ref, wu_ref, wd_ref, ys_ref, *, ff_chunks):
  del be_ref
  i = pl.program_id(0)

  @pl.when(i < used_ref[0])
  def _():
    x = xs_ref[...].astype(BF16)
    d_ff = wg_ref.shape[1]
    cw = d_ff // ff_chunks
    y = None
    for c in range(ff_chunks):
      sl = slice(c * cw, (c + 1) * cw)
      g = _dot(x, wg_ref[:, sl])
      u = _dot(x, wu_ref[:, sl])
      part = _dot((g * jax.nn.sigmoid(g) * u).astype(BF16), wd_ref[sl, :])
      y = part if y is None else y + part
    ys_ref[...] = y

  @pl.when(i >= used_ref[0])
  def _():
    ys_ref[...] = jnp.zeros(ys_ref.shape, ys_ref.dtype)


def _expert_call(t, block_expert, n_used, xs, wg, wu, wd):
  n_slots, d = xs.shape
  tm = t.expert
  d_ff = wg.shape[2]
  assert d_ff % (t.ff_chunks * V7X_LANES) == 0
  row_spec = pl.BlockSpec((tm, d), lambda i, be, used: (i, 0))
  used_row_spec = pl.BlockSpec((tm, d), lambda i, be, used: (jnp.minimum(i, used[0] - 1), 0))
  once = pl.Buffered(1)
  grid_spec = pltpu.PrefetchScalarGridSpec(
      num_scalar_prefetch=2,
      grid=(n_slots // tm,),
      in_specs=[used_row_spec,
                pl.BlockSpec((None, d, d_ff), lambda i, be, used: (be[i], 0, 0),
                             pipeline_mode=once),
                pl.BlockSpec((None, d, d_ff), lambda i, be, used: (be[i], 0, 0),
                             pipeline_mode=once),
                pl.BlockSpec((None, d_ff, d), lambda i, be, used: (be[i], 0, 0),
                             pipeline_mode=once)],
      out_specs=row_spec,
  )
  return pl.pallas_call(
      functools.partial(_expert_kernel, ff_chunks=t.ff_chunks),
      grid_spec=grid_spec,
      out_shape=jax.ShapeDtypeStruct((n_slots, d), F32),
      compiler_params=_params(t, ("arbitrary",)),
      name="moe_experts",
  )(block_expert, n_used, xs, wg, wu, wd)


def _combine_kernel(run_ref, slot_ref, total_ref, ys_hbm, h1_ref, ids_ref, gates_ref, p_ref,
                    ln_ple_ref, wpg_ref, wpp_ref, ln_final_ref, out_ref, y_scr, sems,
                    *, n_tiles, final_norm):
  s = pl.program_id(0)
  pairs = y_scr.shape[1]

  @pl.when(s < n_tiles)
  def _():
    buf = s % 2

    def make_copy(tile_row, slot_row, rows):
      return pltpu.make_async_copy(ys_hbm.at[pl.ds(slot_row, rows), :],
                                   y_scr.at[buf, pl.ds(tile_row, rows), :], sems.at[buf])

    _run_copies(run_ref, slot_ref, s, make_copy, h1_ref.shape[0])

  @pl.when(s > 0)
  def _():
    buf = (s - 1) % 2
    _wait_rows(ys_hbm, y_scr.at[buf], total_ref[s - 1], sems.at[buf])
    fetched = lax.broadcasted_iota(jnp.int32, (pairs, 1), 0) < total_ref[s - 1]
    y = jnp.where(fetched, y_scr[buf], 0.0).astype(BF16)
    ids = ids_ref[...]
    gates = gates_ref[...]
    order = lax.broadcasted_iota(jnp.int32, (ids.shape[0], pairs), 1)
    y1 = _dot(jnp.where(order == ids[:, 2:3], 1.0, 0.0).astype(BF16), y)
    y2 = _dot(jnp.where(order == ids[:, 3:4], 1.0, 0.0).astype(BF16), y)
    h2 = h1_ref[...] + (gates[:, 0:1] * y1 + gates[:, 1:2] * y2)
    h3 = _ple(h2, p_ref[...], ln_ple_ref[...], wpg_ref[...], wpp_ref[...])
    if final_norm:
      h3 = _rms(h3, ln_final_ref[...], NORM_EPS)
    out_ref[...] = h3


def _combine_call(t, tile_run, tile_slot, tile_total, ys, h1, ids, gates, p, ln_ple, wpg, wpp, ln_final,
                  final_norm):
  n, d = h1.shape
  tm = t.tok
  n_tiles = n // tm
  d_ple = p.shape[1]
  prev = lambda s, *_: (jnp.maximum(s - 1, 0), 0)
  const = lambda s, *_: (0, 0)
  once = pl.Buffered(1)
  grid_spec = pltpu.PrefetchScalarGridSpec(
      num_scalar_prefetch=3,
      grid=(n_tiles + 1,),
      in_specs=[pl.BlockSpec(memory_space=pl.ANY),
                pl.BlockSpec((tm, d), prev), pl.BlockSpec((tm, V7X_LANES), prev),
                pl.BlockSpec((tm, V7X_LANES), prev), pl.BlockSpec((tm, d_ple), prev),
                pl.BlockSpec((1, d), const, pipeline_mode=once),
                pl.BlockSpec((d, d), const, pipeline_mode=once),
                pl.BlockSpec((d_ple, d), const, pipeline_mode=once),
                pl.BlockSpec((1, d), const, pipeline_mode=once)],
      out_specs=pl.BlockSpec((tm, d), prev),
      scratch_shapes=[pltpu.VMEM((2, t.pair_rows, d), F32), pltpu.SemaphoreType.DMA((2,))],
  )
  return pl.pallas_call(
      functools.partial(_combine_kernel, n_tiles=n_tiles, final_norm=final_norm),
      grid_spec=grid_spec,
      out_shape=jax.ShapeDtypeStruct((n, d), F32),
      compiler_params=_params(t, ("arbitrary",)),
      name="moe_combine_ple",
  )(tile_run, tile_slot, tile_total, ys, h1, ids, gates, p, ln_ple, wpg, wpp, ln_final)


def _rope_tables(seq):
  inv = 1.0 / (ROPE_THETA ** (jnp.arange(0, DIFF_HEAD_DIM, 2, dtype=F32) / DIFF_HEAD_DIM))
  ang = jnp.arange(seq, dtype=F32)[:, None] * inv[None, :]
  cos, sin = jnp.cos(ang), jnp.sin(ang)
  reps = HEAD_WIDTH // DIFF_HEAD_DIM
  cos_t = jnp.tile(jnp.concatenate([cos, cos], axis=1), (1, reps))
  sin_t = jnp.tile(jnp.concatenate([-sin, sin], axis=1), (1, reps))
  return cos_t, sin_t


def _slot_layout(t, cnt_f):
  tile_cnt = cnt_f[:, 0, :N_EXPERTS].astype(jnp.int32)
  tile_run = (tile_cnt + V7X_SUBLANES - 1) // V7X_SUBLANES * V7X_SUBLANES
  seg_len = jnp.sum(tile_run, axis=0)
  padded = (seg_len + t.expert - 1) // t.expert * t.expert
  seg_end = jnp.cumsum(padded)
  seg_start = seg_end - padded
  tile_slot = seg_start[None, :] + jnp.cumsum(tile_run, axis=0) - tile_run
  n_used = seg_end[-1:] // t.expert
  zero_rows = jnp.concatenate([jnp.where(padded > 0, seg_end - t.expert, -1), n_used])
  block_start = jnp.arange(t.n_expert_blocks, dtype=jnp.int32) * t.expert
  block_expert = jnp.minimum(
      jnp.searchsorted(seg_end, block_start, side='right'), N_EXPERTS - 1)
  i32 = lambda v: v.astype(jnp.int32)
  return (i32(tile_run.reshape(-1)), i32(tile_slot.reshape(-1)),
          i32(jnp.sum(tile_run, axis=1)), i32(zero_rows), i32(block_expert), i32(n_used))


def kernel(x, p, ln_mix, ln_ffn, ln_ple, ln_final, ple_w_proj, ple_w_gate, attn_wq, attn_wk, attn_wv, attn_wo, attn_lambda_q1, attn_lambda_k1, attn_lambda_q2, attn_lambda_k2, attn_subln, ffn_w_gate, ffn_w_up, ffn_w_down, lru_w_in_x, lru_w_in_y, lru_conv_w, lru_conv_b, lru_gate_a_w, lru_gate_a_b, lru_gate_x_w, lru_gate_x_b, lru_lambda, lru_w_out, moe_w_router, moe_w_gate, moe_w_up, moe_w_down):
  batch, seq, d = x.shape
  depth = p.shape[0]
  n = batch * seq
  t = _Tiles(batch, seq)
  row = lambda v: v.reshape(1, -1)
  bf = lambda w: w.astype(BF16)
  cos_t, sin_t = _rope_tables(seq)
  n_slots = t.n_expert_blocks * t.expert

  h = x.reshape(n, d)
  for i in range(depth):
    j = i // 2
    p_i = p[i].reshape(n, -1)
    if i % 2 == 0:
      lambda_init = 0.8 - 0.6 * math.exp(-0.3 * i)
      q, k, v = _qkv_call(t, h, row(ln_mix[i]), bf(attn_wq[j]), bf(attn_wk[j]),
                          bf(attn_wv[j]), cos_t, sin_t, seq)
      lam_v = jnp.stack([attn_lambda_q1[j], attn_lambda_k1[j],
                         attn_lambda_q2[j], attn_lambda_k2[j]])
      o = _flash_call(t, q, k, v, lam_v, row(attn_subln[j]), lambda_init, batch, seq)
      h = _dense_tail_call(t, h, o, p_i, bf(attn_wo[j]), row(ln_ffn[i]),
                           bf(ffn_w_gate[j]), bf(ffn_w_up[j]), bf(ffn_w_down[j]),
                           row(ln_ple[i]), bf(ple_w_gate[i]), bf(ple_w_proj[i]))
    else:
      w_r = jnp.pad(moe_w_router[j], ((0, 0), (0, V7X_LANES - N_EXPERTS)))
      w_r_hi = bf(w_r)
      w_router = jnp.concatenate([w_r_hi, bf(w_r - w_r_hi.astype(F32))], axis=1)
      w_gates = bf(jnp.concatenate([lru_gate_a_w[j], lru_gate_x_w[j]], axis=-1))
      h1, xn2, logits = _rec_call(
          t, h, batch, seq, row(ln_mix[i]), bf(lru_w_in_x[j]), bf(lru_w_in_y[j]),
          lru_conv_w[j], row(lru_conv_b[j]), w_gates, row(lru_gate_a_b[j]),
          row(lru_gate_x_b[j]), row(lru_lambda[j]), bf(lru_w_out[j]), row(ln_ffn[i]),
          w_router)
      ids, gates, pos_t, cnt = _route_call(t, logits)
      tile_run, tile_slot, tile_total, zero_rows, block_expert, n_used = _slot_layout(t, cnt)
      xs = _dispatch_call(t, tile_run, tile_slot, tile_total, zero_rows, xn2, pos_t, n_slots)
      ys = _expert_call(t, block_expert, n_used, xs, bf(moe_w_gate[j]), bf(moe_w_up[j]),
                        bf(moe_w_down[j]))
      h = _combine_call(t, tile_run, tile_slot, tile_total, ys, h1, ids, gates, p_i,
                        row(ln_ple[i]), bf(ple_w_gate[i]), bf(ple_w_proj[i]), row(ln_final),
                        final_norm=(i == depth - 1))
  if depth % 2 == 1:
    raise NotImplementedError("final norm is fused into the last (odd) layer")
  return h.reshape(batch, seq, d)
```

```python
import functools
import math

import jax
import jax.numpy as jnp
from jax import lax
from jax.experimental import pallas as pl
from jax.experimental.pallas import tpu as pltpu

N_DIFF_HEADS = 8
DIFF_HEAD_DIM = 64
HEAD_WIDTH = 2 * DIFF_HEAD_DIM
ROPE_THETA = 10000.0
N_LRU_BLOCKS = 8
LRU_BLOCK = 128
CONV_WIDTH = 4
LRU_C = 8.0
N_EXPERTS = 8
NORM_EPS = 1e-6
SUBLN_EPS = 1e-5

V7X_LANES = 128
V7X_SUBLANES = 8
V7X_VMEM_BYTES = 64 * 1024 * 1024

F32 = jnp.float32
BF16 = jnp.bfloat16


class _Tiles:
  def __init__(self, batch, seq):
    self.tok = min(512, seq)
    self.q = min(512, seq)
    self.expert = 512
    self.ff_chunks = 2
    self.heads_per_step = 4
    self.attn_chunk = 256
    self.rec_streams = 2
    n_tok = batch * seq
    assert seq % self.tok == 0 and seq % self.q == 0
    run_pad = N_EXPERTS * (V7X_SUBLANES - 1)
    self.pair_rows = 2 * self.tok + 64
    assert self.pair_rows >= 2 * self.tok + run_pad and self.expert <= self.pair_rows
    max_slots = 2 * n_tok + (n_tok // self.tok) * run_pad
    self.n_expert_blocks = -(-max_slots // self.expert) + N_EXPERTS
    self.vmem_limit = V7X_VMEM_BYTES - 8 * 1024 * 1024


def _resident(shape):
  zeros = (0,) * len(shape)
  return pl.BlockSpec(shape, lambda *_: zeros, pipeline_mode=pl.Buffered(1))


def _rms(x, g, eps):
  ms = jnp.mean(x * x, axis=-1, keepdims=True)
  return x * lax.rsqrt(ms + eps) * g


def _dot(a, b):
  return jnp.dot(a, b, preferred_element_type=F32)


def _params(t, semantics):
  return pltpu.CompilerParams(dimension_semantics=semantics,
                              vmem_limit_bytes=t.vmem_limit)


def _qkv_kernel(h_ref, g_ref, wq_ref, wk_ref, wv_ref, cos_ref, sin_ref,
                q_ref, k_ref, v_ref):
  xn = _rms(h_ref[...], g_ref[...], NORM_EPS).astype(BF16)
  cos = cos_ref[...]
  sin = sin_ref[...]
  lane = lax.broadcasted_iota(jnp.int32, cos.shape, 1)
  low_half = (lane % DIFF_HEAD_DIM) < (DIFF_HEAD_DIM // 2)

  def rope_store(y, out_ref, scale):
    for hd in range(N_DIFF_HEADS):
      sl = slice(hd * HEAD_WIDTH, (hd + 1) * HEAD_WIDTH)
      ys = y[:, sl]
      partner = jnp.where(low_half,
                          pltpu.roll(ys, HEAD_WIDTH - DIFF_HEAD_DIM // 2, 1),
                          pltpu.roll(ys, DIFF_HEAD_DIM // 2, 1))
      out_ref[:, sl] = ((ys * cos + partner * sin) * scale).astype(out_ref.dtype)

  rope_store(_dot(xn, wq_ref[...]), q_ref, DIFF_HEAD_DIM ** -0.5 * math.log2(math.e))
  rope_store(_dot(xn, wk_ref[...]), k_ref, 1.0)
  v_ref[...] = _dot(xn, wv_ref[...]).astype(v_ref.dtype)


def _qkv_call(t, h, g, wq, wk, wv, cos_t, sin_t, seq):
  n, d = h.shape
  tm = t.tok
  pos_blocks = seq // tm
  tok_spec = pl.BlockSpec((tm, d), lambda i: (i, 0))
  tab_spec = pl.BlockSpec((tm, HEAD_WIDTH), lambda i: (i % pos_blocks, 0))
  out = jax.ShapeDtypeStruct((n, d), BF16)
  return pl.pallas_call(
      _qkv_kernel,
      grid=(n // tm,),
      in_specs=[tok_spec, _resident((1, d)), _resident((d, d)), _resident((d, d)),
                _resident((d, d)), tab_spec, tab_spec],
      out_specs=[tok_spec, tok_spec, tok_spec],
      out_shape=[out, out, out],
      compiler_params=_params(t, ("parallel",)),
      name="qkv_rope",
  )(h, g, wq, wk, wv, cos_t, sin_t)


def _flash_kernel(q_ref, k_ref, v_ref, lam_ref, g_ref, o_ref, qm_scr, m_scr, acc_scr,
                  s_scr, *, tq, chunk, heads, lambda_init):
  i = pl.program_id(2)
  nq = pl.num_programs(2)
  par = i % 2
  lane = lax.broadcasted_iota(jnp.int32, (tq, HEAD_WIDTH), 1)

  def stage_queries(block, parity):
    start = pl.multiple_of(block * tq, tq)
    for hd in range(heads):
      q = q_ref[pl.ds(start, tq), hd * HEAD_WIDTH:(hd + 1) * HEAD_WIDTH]
      zero = jnp.zeros_like(q)
      qm_scr[parity, hd, 0:tq, :] = jnp.where(lane < DIFF_HEAD_DIM, q, zero)
      qm_scr[parity, hd, tq:2 * tq, :] = jnp.where(lane >= DIFF_HEAD_DIM, q, zero)

  m_scr[...] = jnp.full(m_scr.shape, -jnp.inf, F32)
  acc_scr[...] = jnp.zeros(acc_scr.shape, F32)
  ones = jnp.ones((tq, HEAD_WIDTH), BF16)

  n_chunks = 2 * tq // chunk

  def pipeline_step(*, consume=None, produce=None, parity=None, causal=False,
                    diagonal=False):
    if consume is not None:
      v_start = pl.multiple_of(consume * tq, tq)
    if produce is not None:
      k_start = pl.multiple_of(produce * tq, tq)
    for hd in range(heads):
      head = slice(hd * HEAD_WIDTH, (hd + 1) * HEAD_WIDTH)
      if consume is not None:
        v1 = jnp.concatenate([v_ref[pl.ds(v_start, tq), head], ones], axis=1)
      if produce is not None:
        kb = k_ref[pl.ds(k_start, tq), head]
      for c in range(n_chunks):
        rows = slice(c * chunk, (c + 1) * chunk)
        if consume is not None:
          keys = (c * chunk) % tq + chunk if diagonal else tq
          s = s_scr[hd, rows, 0:keys]
          m_prev = m_scr[hd, rows, :]
          m_new = jnp.maximum(m_prev, jnp.max(s, axis=1, keepdims=True))
          alpha = jnp.exp2(m_prev - m_new)
          p = jnp.exp2(s - jnp.concatenate([m_new] * (keys // V7X_LANES), axis=1))
          acc_scr[hd, rows, :] = (
              jnp.concatenate([alpha, alpha], axis=1) * acc_scr[hd, rows, :]
              + _dot(p.astype(BF16), v1[0:keys]))
          m_scr[hd, rows, :] = m_new
        if produce is not None:
          s_new = lax.dot_general(qm_scr[parity, hd, rows, :], kb,
                                  (((1,), (1,)), ((), ())), preferred_element_type=F32)
          if causal:
            row = (lax.broadcasted_iota(jnp.int32, s_new.shape, 0) + c * chunk) % tq
            col = lax.broadcasted_iota(jnp.int32, s_new.shape, 1)
            s_new = jnp.where(col <= row, s_new, -jnp.inf)
          s_scr[hd, rows, :] = s_new

  @pl.when(i == 0)
  def _():
    stage_queries(0, 0)
    pipeline_step(produce=0, parity=0, causal=True)

  def body(j, carry):
    pipeline_step(consume=j, produce=j + 1, parity=par)
    return carry

  lax.fori_loop(0, i - 1, body, 0)

  @pl.when(i > 0)
  def _():
    pipeline_step(consume=i - 1, produce=i, parity=par, causal=True)

  stage_queries(jnp.minimum(i + 1, nq - 1), 1 - par)
  pipeline_step(consume=i, produce=0, parity=1 - par, diagonal=True)

  lam_v = lam_ref[...]
  lam = (jnp.exp(jnp.sum(lam_v[0:1] * lam_v[1:2], axis=1, keepdims=True))
         - jnp.exp(jnp.sum(lam_v[2:3] * lam_v[3:4], axis=1, keepdims=True))
         + lambda_init)
  for hd in range(heads):
    acc = acc_scr[hd]
    o_all = acc[:, :HEAD_WIDTH] / acc[:, HEAD_WIDTH:]
    o = o_all[:tq] - lam * o_all[tq:]
    o_ref[:, hd * HEAD_WIDTH:(hd + 1) * HEAD_WIDTH] = (
        _rms(o, g_ref[...], SUBLN_EPS) * (1.0 - lambda_init)).astype(o_ref.dtype)


def _flash_call(t, q, k, v, lam_v, subln_g, lambda_init, batch, seq):
  n, d = q.shape
  tq = t.q
  nq = seq // tq
  heads = t.heads_per_step
  width = heads * HEAD_WIDTH
  out_spec = pl.BlockSpec((tq, width), lambda b, h, i: (b * nq + i, h))
  seq_spec = pl.BlockSpec((seq, width), lambda b, h, i: (b, h))
  return pl.pallas_call(
      functools.partial(_flash_kernel, tq=tq, chunk=min(t.attn_chunk, tq), heads=heads,
                        lambda_init=lambda_init),
      grid=(batch, N_DIFF_HEADS // heads, nq),
      in_specs=[seq_spec, seq_spec, seq_spec,
                pl.BlockSpec(lam_v.shape, lambda b, h, i: (0, 0)),
                pl.BlockSpec(subln_g.shape, lambda b, h, i: (0, 0))],
      out_specs=out_spec,
      out_shape=jax.ShapeDtypeStruct((n, d), BF16),
      scratch_shapes=[pltpu.VMEM((2, heads, 2 * tq, HEAD_WIDTH), BF16),
                      pltpu.VMEM((heads, 2 * tq, HEAD_WIDTH), F32),
                      pltpu.VMEM((heads, 2 * tq, 2 * HEAD_WIDTH), F32),
                      pltpu.VMEM((heads, 2 * tq, tq), F32)],
      compiler_params=_params(t, ("parallel", "parallel", "arbitrary")),
      name="diff_flash",
  )(q, k, v, lam_v, subln_g)


def _ple(h, p, ln_g, w_gate, w_proj):
  gate = jax.nn.sigmoid(_dot(_rms(h, ln_g, NORM_EPS).astype(BF16), w_gate))
  return h + gate * _dot(p.astype(BF16), w_proj)


def _dense_tail_kernel(h_ref, o_ref, p_ref, wo_ref, ln_ffn_ref, wg_ref, wu_ref, wd_ref,
                       ln_ple_ref, wpg_ref, wpp_ref, out_ref, *, ff_chunks):
  h1 = h_ref[...] + _dot(o_ref[...], wo_ref[...])
  xn = _rms(h1, ln_ffn_ref[...], NORM_EPS).astype(BF16)
  d_ff = wg_ref.shape[1]
  cw = d_ff // ff_chunks
  y = None
  for c in range(ff_chunks):
    sl = slice(c * cw, (c + 1) * cw)
    g = _dot(xn, wg_ref[:, sl])
    u = _dot(xn, wu_ref[:, sl])
    part = _dot((g * jax.nn.sigmoid(g) * u).astype(BF16), wd_ref[sl, :])
    y = part if y is None else y + part
  h2 = h1 + y
  out_ref[...] = _ple(h2, p_ref[...], ln_ple_ref[...], wpg_ref[...], wpp_ref[...])


def _dense_tail_call(t, h, o, p, layer, wo, ln_ffn, wg, wu, wd, ln_ple, wpg, wpp):
  n, d = h.shape
  tm = t.tok
  d_ff = wg.shape[1]
  d_ple = p.shape[2]
  assert d_ff % (t.ff_chunks * V7X_LANES) == 0
  tok_spec = pl.BlockSpec((tm, d), lambda i: (i, 0))
  return pl.pallas_call(
      functools.partial(_dense_tail_kernel, ff_chunks=t.ff_chunks),
      grid=(n // tm,),
      in_specs=[tok_spec, tok_spec,
                pl.BlockSpec((None, tm, d_ple), lambda i: (layer, i, 0)),
                _resident((d, d)), _resident((1, d)), _resident((d, d_ff)),
                _resident((d, d_ff)), _resident((d_ff, d)), _resident((1, d)),
                _resident((d, d)), _resident((d_ple, d))],
      out_specs=tok_spec,
      out_shape=jax.ShapeDtypeStruct((n, d), F32),
      compiler_params=_params(t, ("parallel",)),
      name="dense_tail",
  )(h, o, p, wo, ln_ffn, wg, wu, wd, ln_ple, wpg, wpp)


def _rec_kernel(h_ref, ln_mix_ref, wx_ref, wy_ref, convw_ref, convb_ref, wgate_ref, ba_ref,
                bi_ref, lam_ref, wout_ref, ln_ffn_ref, wr_ref,
                h1_ref, xn2_ref, logit_ref,
                xs_scr, a_scr, b_scr, hs_scr, carry_scr):
  tt = pl.program_id(1)
  streams, ts, width = h_ref.shape
  pad = V7X_SUBLANES

  @pl.when(tt == 0)
  def _():
    xs_scr[:, 0:pad, :] = jnp.zeros((streams, pad, width), F32)
    carry_scr[...] = jnp.zeros(carry_scr.shape, F32)

  neg_lam = -lam_ref[...]
  softplus = jnp.maximum(neg_lam, 0.0) + jnp.log1p(jnp.exp(-jnp.abs(neg_lam)))
  cw = convw_ref[...]

  def gate_stage(st):
    x = h_ref[st]
    xn = _rms(x, ln_mix_ref[...], NORM_EPS).astype(BF16)
    yv = _dot(xn, wy_ref[...])
    yb = yv * jax.nn.sigmoid(
        yv * (2.0 * math.sqrt(2.0 / math.pi) * (1.0 + 0.044715 * (yv * yv))))
    xs_scr[st, pad:pad + ts, :] = _dot(xn, wx_ref[...])

    u = convb_ref[...] + cw[CONV_WIDTH - 1:CONV_WIDTH] * xs_scr[st, pad:pad + ts, :]
    for back in range(1, CONV_WIDTH):
      tap = CONV_WIDTH - 1 - back
      u = u + cw[tap:tap + 1] * xs_scr[st, pad - back:pad - back + ts, :]
    xs_scr[st, 0:pad, :] = xs_scr[st, ts:ts + pad, :]

    ub = u.astype(BF16)
    gates = [_dot(ub[:, nb * LRU_BLOCK:(nb + 1) * LRU_BLOCK], wgate_ref[nb])
             for nb in range(N_LRU_BLOCKS)]
    r = jax.nn.sigmoid(
        jnp.concatenate([g[:, :LRU_BLOCK] for g in gates], axis=1) + ba_ref[...])
    gate_i = jax.nn.sigmoid(
        jnp.concatenate([g[:, LRU_BLOCK:] for g in gates], axis=1) + bi_ref[...])
    log_a = (-LRU_C) * r * softplus
    a = jnp.exp(log_a)
    one_m_a2 = -jnp.tanh(log_a) * (a * a + 1.0)
    mult = jnp.where(one_m_a2 > 0.0, one_m_a2 * lax.rsqrt(one_m_a2), 0.0)
    a_scr[st] = a
    b_scr[st] = mult * gate_i * u
    return x, yb

  staged = [gate_stage(st) for st in range(streams)]

  row = lax.broadcasted_iota(jnp.int32, (V7X_SUBLANES, width), 0)

  def scan_group(gi, carries):
    r0 = pl.multiple_of(gi * V7X_SUBLANES, V7X_SUBLANES)
    out = []
    for st in range(streams):
      av = a_scr[st, pl.ds(r0, V7X_SUBLANES), :]
      bv = b_scr[st, pl.ds(r0, V7X_SUBLANES), :]
      for sh in (1, 2, 4):
        keep = row >= sh
        a_prev = jnp.where(keep, pltpu.roll(av, sh, 0), 1.0)
        b_prev = jnp.where(keep, pltpu.roll(bv, sh, 0), 0.0)
        bv = av * b_prev + bv
        av = av * a_prev
      hv = av * carries[st] + bv
      hs_scr[st, pl.ds(r0, V7X_SUBLANES), :] = hv
      out.append(jnp.broadcast_to(hv[V7X_SUBLANES - 1:V7X_SUBLANES, :], hv.shape))
    return tuple(out)

  carries = lax.fori_loop(
      0, ts // V7X_SUBLANES, scan_group,
      tuple(jnp.broadcast_to(carry_scr[st], row.shape) for st in range(streams)))

  for st in range(streams):
    x, yb = staged[st]
    carry_scr[st] = carries[st][0:1, :]
    h1 = x + _dot((hs_scr[st] * yb).astype(BF16), wout_ref[...])
    h1_ref[st] = h1
    xn2 = _rms(h1, ln_ffn_ref[...], NORM_EPS)
    x_hi = xn2.astype(BF16)
    xn2_ref[st] = x_hi
    x_lo = (xn2 - x_hi.astype(F32)).astype(BF16)
    parts = _dot(x_hi, wr_ref[...]) + _dot(x_lo, wr_ref[...])
    logit_ref[st] = parts[:, :V7X_LANES] + parts[:, V7X_LANES:]


def _rec_call(t, h, batch, seq, ln_mix, wx, wy, conv_w, conv_b, w_gates, ba, bi, lam, wout,
              ln_ffn, w_router):
  n, d = h.shape
  ts = t.tok
  streams = t.rec_streams if batch % t.rec_streams == 0 else 1

  def tok_spec(width):
    return pl.BlockSpec((streams, ts, width), lambda b, s: (b, s, 0))

  scr = pltpu.VMEM((streams, ts, d), F32)
  h1, xn2, logits = pl.pallas_call(
      _rec_kernel,
      grid=(batch // streams, seq // ts),
      in_specs=[tok_spec(d), _resident((1, d)), _resident((d, d)), _resident((d, d)),
                _resident(conv_w.shape), _resident((1, d)), _resident(w_gates.shape),
                _resident((1, d)), _resident((1, d)),
                _resident((1, d)), _resident((d, d)), _resident((1, d)),
                _resident(w_router.shape)],
      out_specs=[tok_spec(d), tok_spec(d), tok_spec(V7X_LANES)],
      out_shape=[jax.ShapeDtypeStruct((batch, seq, d), F32),
                 jax.ShapeDtypeStruct((batch, seq, d), BF16),
                 jax.ShapeDtypeStruct((batch, seq, V7X_LANES), F32)],
      scratch_shapes=[pltpu.VMEM((streams, ts + V7X_SUBLANES, d), F32), scr, scr, scr,
                      pltpu.VMEM((streams, 1, d), F32)],
      compiler_params=_params(t, ("parallel", "arbitrary")),
      name="recurrent_block",
  )(h.reshape(batch, seq, d), ln_mix, wx, wy, conv_w, conv_b, w_gates, ba, bi, lam, wout,
    ln_ffn, w_router)
  return h1.reshape(n, d), xn2.reshape(n, d), logits.reshape(n, V7X_LANES)


def _route_kernel(logit_ref, ids_ref, gates_ref, pos_t_ref, cnt_ref):
  lg = logit_ref[...]
  tr = lg.shape[0]
  lane = lax.broadcasted_iota(jnp.int32, lg.shape, 1)
  lg = jnp.where(lane < N_EXPERTS, lg, -jnp.inf)
  m1 = jnp.max(lg, axis=1, keepdims=True)
  e1 = jnp.min(jnp.where(lg == m1, lane, V7X_LANES), axis=1, keepdims=True)
  lg2 = jnp.where(lane == e1, -jnp.inf, lg)
  m2 = jnp.max(lg2, axis=1, keepdims=True)
  e2 = jnp.min(jnp.where(lg2 == m2, lane, V7X_LANES), axis=1, keepdims=True)
  ex = jnp.exp(m2 - m1)
  g1 = 1.0 / (1.0 + ex)
  g2 = ex / (1.0 + ex)

  chosen = jnp.where((lane == e1) | (lane == e2), 1.0, 0.0)
  r_i = lax.broadcasted_iota(jnp.int32, (tr, tr), 0)
  c_i = lax.broadcasted_iota(jnp.int32, (tr, tr), 1)
  earlier = jnp.where(c_i < r_i, 1.0, 0.0).astype(BF16)
  rank = _dot(earlier, chosen.astype(BF16))
  cnt = jnp.sum(chosen, axis=0, keepdims=True)
  run = jnp.ceil(cnt * (1.0 / V7X_SUBLANES)) * V7X_SUBLANES

  def position(e):
    return jnp.sum(jnp.where(lane < e, run, jnp.where(lane == e, rank, 0.0)),
                   axis=1, keepdims=True).astype(jnp.int32)

  pos1, pos2 = position(e1), position(e2)
  ids_ref[...] = jnp.where(lane == 0, e1, jnp.where(lane == 1, e2, jnp.where(
      lane == 2, pos1, jnp.where(lane == 3, pos2, 0))))
  gates_ref[...] = jnp.where(lane == 0, g1, jnp.where(lane == 1, g2, 0.0))
  pos_cols = jnp.where(lane == 0, pos1, jnp.where(lane == 1, pos2, 0))
  pos_t_ref[0] = jnp.transpose(pos_cols)[0:V7X_SUBLANES, :]
  cnt_ref[0] = jnp.broadcast_to(cnt, cnt_ref.shape[1:])


def _route_call(t, logits):
  n = logits.shape[0]
  tr = t.tok
  spec = pl.BlockSpec((tr, V7X_LANES), lambda i: (i, 0))
  return pl.pallas_call(
      _route_kernel,
      grid=(n // tr,),
      in_specs=[spec],
      out_specs=[spec, spec,
                 pl.BlockSpec((1, V7X_SUBLANES, tr), lambda i: (i, 0, 0)),
                 pl.BlockSpec((1, V7X_SUBLANES, V7X_LANES), lambda i: (i, 0, 0))],
      out_shape=[jax.ShapeDtypeStruct((n, V7X_LANES), jnp.int32),
                 jax.ShapeDtypeStruct((n, V7X_LANES), F32),
                 jax.ShapeDtypeStruct((n // tr, V7X_SUBLANES, tr), jnp.int32),
                 jax.ShapeDtypeStruct((n // tr, V7X_SUBLANES, V7X_LANES), F32)],
      compiler_params=_params(t, ("parallel",)),
      name="route_top2",
  )(logits)


def _run_copies(run_ref, slot_ref, tile, make_copy, max_rows):
  top_bit = max_rows.bit_length() - 1
  low_bit = V7X_SUBLANES.bit_length() - 1
  assert max_rows == 1 << top_bit
  tile_row = 0
  for e in range(N_EXPERTS):
    run = run_ref[tile * N_EXPERTS + e]
    slot_row = slot_ref[tile * N_EXPERTS + e]
    for k in range(top_bit, low_bit - 1, -1):
      done = (run >> (k + 1)) << (k + 1)

      @pl.when(((run >> k) & 1) == 1)
      def _():
        make_copy(pl.multiple_of(tile_row + done, V7X_SUBLANES),
                  pl.multiple_of(slot_row + done, V7X_SUBLANES), 1 << k).start()
    tile_row = tile_row + run


def _wait_rows(hbm, vmem_buf, rows, sem):
  rows = pl.multiple_of(rows, V7X_SUBLANES)
  pltpu.make_async_copy(hbm.at[pl.ds(0, rows), :], vmem_buf.at[pl.ds(0, rows), :], sem).wait()


def _dispatch_kernel(run_ref, slot_ref, total_ref, zero_ref, x_ref, pos_t_ref, xs_hbm,
                     y_scr, sems, *, n_tiles, expert_rows):
  i = pl.program_id(0)
  buf = i % 2

  @pl.when(i == 0)
  def _():
    y_scr[1, 0:expert_rows, :] = jnp.zeros((expert_rows, y_scr.shape[2]), y_scr.dtype)

    def zero_copy(e):
      first = pl.multiple_of(zero_ref[e], expert_rows)
      return pltpu.make_async_copy(y_scr.at[1, pl.ds(0, expert_rows), :],
                                   xs_hbm.at[pl.ds(first, expert_rows), :], sems.at[1])
    for e in range(N_EXPERTS):
      @pl.when(zero_ref[e] >= 0)
      def _():
        zero_copy(e).start()
    for e in range(N_EXPERTS):
      @pl.when(zero_ref[e] >= 0)
      def _():
        zero_copy(e).wait()

    def zero_block(b, carry):
      first = pl.multiple_of(b * expert_rows, expert_rows)
      cp = pltpu.make_async_copy(y_scr.at[1, pl.ds(0, expert_rows), :],
                                 xs_hbm.at[pl.ds(first, expert_rows), :], sems.at[1])
      cp.start()
      cp.wait()
      return carry

    lax.fori_loop(zero_ref[N_EXPERTS], xs_hbm.shape[0] // expert_rows, zero_block, 0)

  pos = pos_t_ref[0]
  order = lax.broadcasted_iota(jnp.int32, (y_scr.shape[1], x_ref.shape[0]), 0)
  select = jnp.where((order == pos[0:1, :]) | (order == pos[1:2, :]), 1.0, 0.0)
  y_scr[buf] = _dot(select.astype(BF16), x_ref[...])

  def make_copy(tile_row, slot_row, rows):
    return pltpu.make_async_copy(y_scr.at[buf, pl.ds(tile_row, rows), :],
                                 xs_hbm.at[pl.ds(slot_row, rows), :], sems.at[buf])

  _run_copies(run_ref, slot_ref, i, make_copy, x_ref.shape[0])

  @pl.when(i > 0)
  def _():
    _wait_rows(xs_hbm, y_scr.at[1 - buf], total_ref[i - 1], sems.at[1 - buf])

  @pl.when(i == n_tiles - 1)
  def _():
    _wait_rows(xs_hbm, y_scr.at[buf], total_ref[i], sems.at[buf])


def _dispatch_call(t, tile_run, tile_slot, tile_total, zero_rows, x, pos_t, n_slots):
  n, d = x.shape
  tm = t.tok
  n_tiles = n // tm
  grid_spec = pltpu.PrefetchScalarGridSpec(
      num_scalar_prefetch=4,
      grid=(n_tiles,),
      in_specs=[pl.BlockSpec((tm, d), lambda i, *_: (i, 0)),
                pl.BlockSpec((1, V7X_SUBLANES, tm), lambda i, *_: (i, 0, 0))],
      out_specs=pl.BlockSpec(memory_space=pl.ANY),
      scratch_shapes=[pltpu.VMEM((2, t.pair_rows, d), F32), pltpu.SemaphoreType.DMA((2,))],
  )
  return pl.pallas_call(
      functools.partial(_dispatch_kernel, n_tiles=n_tiles, expert_rows=t.expert),
      grid_spec=grid_spec,
      out_shape=jax.ShapeDtypeStruct((n_slots, d), F32),
      compiler_params=_params(t, ("arbitrary",)),
      name="moe_dispatch",
  )(tile_run, tile_slot, tile_total, zero_rows, x, pos_t)


def _expert_kernel(be_ref, used_ref, xs_ref, wg_ref, wu_ref, wd_ref, ys_ref, *, ff_chunks):
  del be_ref
  i = pl.program_id(0)

  @pl.when(i < used_ref[0])
  def _():
    x = xs_ref[...].astype(BF16)
    d_ff = wg_ref.shape[1]
    cw = d_ff // ff_chunks
    y = None
    for c in range(ff_chunks):
      sl = slice(c * cw, (c + 1) * cw)
      g = _dot(x, wg_ref[:, sl])
      u = _dot(x, wu_ref[:, sl])
      part = _dot((g * jax.nn.sigmoid(g) * u).astype(BF16), wd_ref[sl, :])
      y = part if y is None else y + part
    ys_ref[...] = y

  @pl.when(i >= used_ref[0])
  def _():
    ys_ref[...] = jnp.zeros(ys_ref.shape, ys_ref.dtype)


def _expert_call(t, block_expert, n_used, xs, wg, wu, wd):
  n_slots, d = xs.shape
  tm = t.expert
  d_ff = wg.shape[2]
  assert d_ff % (t.ff_chunks * V7X_LANES) == 0
  row_spec = pl.BlockSpec((tm, d), lambda i, be, used: (i, 0))
  used_row_spec = pl.BlockSpec((tm, d), lambda i, be, used: (jnp.minimum(i, used[0] - 1), 0))
  once = pl.Buffered(1)
  grid_spec = pltpu.PrefetchScalarGridSpec(
      num_scalar_prefetch=2,
      grid=(n_slots // tm,),
      in_specs=[used_row_spec,
                pl.BlockSpec((None, d, d_ff), lambda i, be, used: (be[i], 0, 0),
                             pipeline_mode=once),
                pl.BlockSpec((None, d, d_ff), lambda i, be, used: (be[i], 0, 0),
                             pipeline_mode=once),
                pl.BlockSpec((None, d_ff, d), lambda i, be, used: (be[i], 0, 0),
                             pipeline_mode=once)],
      out_specs=row_spec,
  )
  return pl.pallas_call(
      functools.partial(_expert_kernel, ff_chunks=t.ff_chunks),
      grid_spec=grid_spec,
      out_shape=jax.ShapeDtypeStruct((n_slots, d), F32),
      compiler_params=_params(t, ("arbitrary",)),
      name="moe_experts",
  )(block_expert, n_used, xs, wg, wu, wd)


def _combine_kernel(run_ref, slot_ref, total_ref, ys_hbm, h1_ref, ids_ref, gates_ref, p_ref,
                    ln_ple_ref, wpg_ref, wpp_ref, ln_final_ref, out_ref, y_scr, sems,
                    *, n_tiles, final_norm):
  s = pl.program_id(0)
  pairs = y_scr.shape[1]

  @pl.when(s < n_tiles)
  def _():
    buf = s % 2

    def make_copy(tile_row, slot_row, rows):
      return pltpu.make_async_copy(ys_hbm.at[pl.ds(slot_row, rows), :],
                                   y_scr.at[buf, pl.ds(tile_row, rows), :], sems.at[buf])

    _run_copies(run_ref, slot_ref, s, make_copy, h1_ref.shape[0])

  @pl.when(s > 0)
  def _():
    buf = (s - 1) % 2
    _wait_rows(ys_hbm, y_scr.at[buf], total_ref[s - 1], sems.at[buf])
    fetched = lax.broadcasted_iota(jnp.int32, (pairs, 1), 0) < total_ref[s - 1]
    y = jnp.where(fetched, y_scr[buf], 0.0).astype(BF16)
    ids = ids_ref[...]
    gates = gates_ref[...]
    order = lax.broadcasted_iota(jnp.int32, (ids.shape[0], pairs), 1)
    y1 = _dot(jnp.where(order == ids[:, 2:3], 1.0, 0.0).astype(BF16), y)
    y2 = _dot(jnp.where(order == ids[:, 3:4], 1.0, 0.0).astype(BF16), y)
    h2 = h1_ref[...] + (gates[:, 0:1] * y1 + gates[:, 1:2] * y2)
    h3 = _ple(h2, p_ref[...], ln_ple_ref[...], wpg_ref[...], wpp_ref[...])
    if final_norm:
      h3 = _rms(h3, ln_final_ref[...], NORM_EPS)
    out_ref[...] = h3


def _combine_call(t, tile_run, tile_slot, tile_total, ys, h1, ids, gates, p, layer, ln_ple,
                  wpg, wpp, ln_final, final_norm):
  n, d = h1.shape
  tm = t.tok
  n_tiles = n // tm
  d_ple = p.shape[2]
  prev = lambda s, *_: (jnp.maximum(s - 1, 0), 0)
  const = lambda s, *_: (0, 0)
  once = pl.Buffered(1)
  grid_spec = pltpu.PrefetchScalarGridSpec(
      num_scalar_prefetch=3,
      grid=(n_tiles + 1,),
      in_specs=[pl.BlockSpec(memory_space=pl.ANY),
                pl.BlockSpec((tm, d), prev), pl.BlockSpec((tm, V7X_LANES), prev),
                pl.BlockSpec((tm, V7X_LANES), prev),
                pl.BlockSpec((None, tm, d_ple),
                             lambda s, *_: (layer, jnp.maximum(s - 1, 0), 0)),
                pl.BlockSpec((1, d), const, pipeline_mode=once),
                pl.BlockSpec((d, d), const, pipeline_mode=once),
                pl.BlockSpec((d_ple, d), const, pipeline_mode=once),
                pl.BlockSpec((1, d), const, pipeline_mode=once)],
      out_specs=pl.BlockSpec((tm, d), prev),
      scratch_shapes=[pltpu.VMEM((2, t.pair_rows, d), F32), pltpu.SemaphoreType.DMA((2,))],
  )
  return pl.pallas_call(
      functools.partial(_combine_kernel, n_tiles=n_tiles, final_norm=final_norm),
      grid_spec=grid_spec,
      out_shape=jax.ShapeDtypeStruct((n, d), F32),
      compiler_params=_params(t, ("arbitrary",)),
      name="moe_combine_ple",
  )(tile_run, tile_slot, tile_total, ys, h1, ids, gates, p, ln_ple, wpg, wpp, ln_final)


def _rope_tables(seq):
  inv = 1.0 / (ROPE_THETA ** (jnp.arange(0, DIFF_HEAD_DIM, 2, dtype=F32) / DIFF_HEAD_DIM))
  ang = jnp.arange(seq, dtype=F32)[:, None] * inv[None, :]
  cos, sin = jnp.cos(ang), jnp.sin(ang)
  reps = HEAD_WIDTH // DIFF_HEAD_DIM
  cos_t = jnp.tile(jnp.concatenate([cos, cos], axis=1), (1, reps))
  sin_t = jnp.tile(jnp.concatenate([-sin, sin], axis=1), (1, reps))
  return cos_t, sin_t


def _slot_layout(t, cnt_f):
  tile_cnt = cnt_f[:, 0, :N_EXPERTS].astype(jnp.int32)
  tile_run = (tile_cnt + V7X_SUBLANES - 1) // V7X_SUBLANES * V7X_SUBLANES
  seg_len = jnp.sum(tile_run, axis=0)
  padded = (seg_len + t.expert - 1) // t.expert * t.expert
  seg_end = jnp.cumsum(padded)
  seg_start = seg_end - padded
  tile_slot = seg_start[None, :] + jnp.cumsum(tile_run, axis=0) - tile_run
  n_used = seg_end[-1:] // t.expert
  zero_rows = jnp.concatenate([jnp.where(padded > 0, seg_end - t.expert, -1), n_used])
  block_start = jnp.arange(t.n_expert_blocks, dtype=jnp.int32) * t.expert
  block_expert = jnp.minimum(
      jnp.searchsorted(seg_end, block_start, side='right'), N_EXPERTS - 1)
  i32 = lambda v: v.astype(jnp.int32)
  return (i32(tile_run.reshape(-1)), i32(tile_slot.reshape(-1)),
          i32(jnp.sum(tile_run, axis=1)), i32(zero_rows), i32(block_expert), i32(n_used))


def kernel(x, p, ln_mix, ln_ffn, ln_ple, ln_final, ple_w_proj, ple_w_gate, attn_wq, attn_wk, attn_wv, attn_wo, attn_lambda_q1, attn_lambda_k1, attn_lambda_q2, attn_lambda_k2, attn_subln, ffn_w_gate, ffn_w_up, ffn_w_down, lru_w_in_x, lru_w_in_y, lru_conv_w, lru_conv_b, lru_gate_a_w, lru_gate_a_b, lru_gate_x_w, lru_gate_x_b, lru_lambda, lru_w_out, moe_w_router, moe_w_gate, moe_w_up, moe_w_down):
  batch, seq, d = x.shape
  depth = p.shape[0]
  n = batch * seq
  t = _Tiles(batch, seq)
  row = lambda v: v.reshape(1, -1)
  bf = lambda w: w.astype(BF16)
  cos_t, sin_t = _rope_tables(seq)
  n_slots = t.n_expert_blocks * t.expert

  h = x.reshape(n, d)
  p_tok = p.reshape(depth, n, -1)
  for i in range(depth):
    j = i // 2
    if i % 2 == 0:
      lambda_init = 0.8 - 0.6 * math.exp(-0.3 * i)
      q, k, v = _qkv_call(t, h, row(ln_mix[i]), bf(attn_wq[j]), bf(attn_wk[j]),
                          bf(attn_wv[j]), cos_t, sin_t, seq)
      lam_v = jnp.stack([attn_lambda_q1[j], attn_lambda_k1[j],
                         attn_lambda_q2[j], attn_lambda_k2[j]])
      o = _flash_call(t, q, k, v, lam_v, row(attn_subln[j]), lambda_init, batch, seq)
      h = _dense_tail_call(t, h, o, p_tok, i, bf(attn_wo[j]), row(ln_ffn[i]),
                           bf(ffn_w_gate[j]), bf(ffn_w_up[j]), bf(ffn_w_down[j]),
                           row(ln_ple[i]), bf(ple_w_gate[i]), bf(ple_w_proj[i]))
    else:
      w_r = jnp.pad(moe_w_router[j], ((0, 0), (0, V7X_LANES - N_EXPERTS)))
      w_r_hi = bf(w_r)
      w_router = jnp.concatenate([w_r_hi, bf(w_r - w_r_hi.astype(F32))], axis=1)
      w_gates = bf(jnp.concatenate([lru_gate_a_w[j], lru_gate_x_w[j]], axis=-1))
      h1, xn2, logits = _rec_call(
          t, h, batch, seq, row(ln_mix[i]), bf(lru_w_in_x[j]), bf(lru_w_in_y[j]),
          lru_conv_w[j], row(lru_conv_b[j]), w_gates, row(lru_gate_a_b[j]),
          row(lru_gate_x_b[j]), row(lru_lambda[j]), bf(lru_w_out[j]), row(ln_ffn[i]),
          w_router)
      ids, gates, pos_t, cnt = _route_call(t, logits)
      tile_run, tile_slot, tile_total, zero_rows, block_expert, n_used = _slot_layout(t, cnt)
      xs = _dispatch_call(t, tile_run, tile_slot, tile_total, zero_rows, xn2, pos_t, n_slots)
      ys = _expert_call(t, block_expert, n_used, xs, bf(moe_w_gate[j]), bf(moe_w_up[j]),
                        bf(moe_w_down[j]))
      h = _combine_call(t, tile_run, tile_slot, tile_total, ys, h1, ids, gates, p_tok, i,
                        row(ln_ple[i]), bf(ple_w_gate[i]), bf(ple_w_proj[i]), row(ln_final),
                        final_norm=(i == depth - 1))
  if depth % 2 == 1:
    raise NotImplementedError("final norm is fused into the last (odd) layer")
  return h.reshape(batch, seq, d)
```

```python
import functools
import math

import jax
import jax.numpy as jnp
from jax import lax
from jax.experimental import pallas as pl
from jax.experimental.pallas import tpu as pltpu

N_DIFF_HEADS = 8
DIFF_HEAD_DIM = 64
HEAD_WIDTH = 2 * DIFF_HEAD_DIM
ROPE_THETA = 10000.0
N_LRU_BLOCKS = 8
LRU_BLOCK = 128
CONV_WIDTH = 4
LRU_C = 8.0
N_EXPERTS = 8
NORM_EPS = 1e-6
SUBLN_EPS = 1e-5

V7X_LANES = 128
V7X_SUBLANES = 8
V7X_VMEM_BYTES = 64 * 1024 * 1024

F32 = jnp.float32
BF16 = jnp.bfloat16


class _Tiles:
  def __init__(self, batch, seq):
    self.tok = min(512, seq)
    self.q = min(512, seq)
    self.expert = 512
    self.ff_chunks = 2
    self.heads_per_step = 4
    self.attn_chunk = 256
    self.rec_streams = 2
    n_tok = batch * seq
    assert seq % self.tok == 0 and seq % self.q == 0
    run_pad = N_EXPERTS * (V7X_SUBLANES - 1)
    self.pair_rows = 2 * self.tok + 64
    assert self.pair_rows >= 2 * self.tok + run_pad and self.expert <= self.pair_rows
    max_slots = 2 * n_tok + (n_tok // self.tok) * run_pad
    self.n_expert_blocks = -(-max_slots // self.expert) + N_EXPERTS
    self.vmem_limit = V7X_VMEM_BYTES - 8 * 1024 * 1024


def _resident(shape):
  zeros = (0,) * len(shape)
  return pl.BlockSpec(shape, lambda *_: zeros, pipeline_mode=pl.Buffered(1))


def _rms(x, g, eps):
  ms = jnp.mean(x * x, axis=-1, keepdims=True)
  return x * lax.rsqrt(ms + eps) * g


def _dot(a, b):
  return jnp.dot(a, b, preferred_element_type=F32)


def _params(t, semantics):
  return pltpu.CompilerParams(dimension_semantics=semantics,
                              vmem_limit_bytes=t.vmem_limit)


def _qkv_kernel(h_ref, g_ref, wq_ref, wk_ref, wv_ref, cos_ref, sin_ref,
                q_ref, k_ref, v_ref):
  xn = _rms(h_ref[...], g_ref[...], NORM_EPS).astype(BF16)
  cos = cos_ref[...]
  sin = sin_ref[...]
  lane = lax.broadcasted_iota(jnp.int32, cos.shape, 1)
  low_half = (lane % DIFF_HEAD_DIM) < (DIFF_HEAD_DIM // 2)

  def rope_store(y, out_ref, scale):
    for hd in range(N_DIFF_HEADS):
      sl = slice(hd * HEAD_WIDTH, (hd + 1) * HEAD_WIDTH)
      ys = y[:, sl]
      partner = jnp.where(low_half,
                          pltpu.roll(ys, HEAD_WIDTH - DIFF_HEAD_DIM // 2, 1),
                          pltpu.roll(ys, DIFF_HEAD_DIM // 2, 1))
      out_ref[:, sl] = ((ys * cos + partner * sin) * scale).astype(out_ref.dtype)

  rope_store(_dot(xn, wq_ref[...]), q_ref, DIFF_HEAD_DIM ** -0.5 * math.log2(math.e))
  rope_store(_dot(xn, wk_ref[...]), k_ref, 1.0)
  v_ref[...] = _dot(xn, wv_ref[...]).astype(v_ref.dtype)


def _qkv_call(t, h, g, wq, wk, wv, cos_t, sin_t, seq):
  n, d = h.shape
  tm = t.tok
  pos_blocks = seq // tm
  tok_spec = pl.BlockSpec((tm, d), lambda i: (i, 0))
  tab_spec = pl.BlockSpec((tm, HEAD_WIDTH), lambda i: (i % pos_blocks, 0))
  out = jax.ShapeDtypeStruct((n, d), BF16)
  return pl.pallas_call(
      _qkv_kernel,
      grid=(n // tm,),
      in_specs=[tok_spec, _resident((1, d)), _resident((d, d)), _resident((d, d)),
                _resident((d, d)), tab_spec, tab_spec],
      out_specs=[tok_spec, tok_spec, tok_spec],
      out_shape=[out, out, out],
      compiler_params=_params(t, ("parallel",)),
      name="qkv_rope",
  )(h, g, wq, wk, wv, cos_t, sin_t)


def _flash_kernel(q_ref, k_ref, v_ref, lam_ref, g_ref, o_ref, qm_scr, m_scr, acc_scr,
                  s_scr, *, tq, chunk, heads, lambda_init):
  i = pl.program_id(2)
  nq = pl.num_programs(2)
  par = i % 2
  lane = lax.broadcasted_iota(jnp.int32, (tq, HEAD_WIDTH), 1)

  def stage_queries(block, parity):
    start = pl.multiple_of(block * tq, tq)
    for hd in range(heads):
      q = q_ref[pl.ds(start, tq), hd * HEAD_WIDTH:(hd + 1) * HEAD_WIDTH]
      zero = jnp.zeros_like(q)
      qm_scr[parity, hd, 0:tq, :] = jnp.where(lane < DIFF_HEAD_DIM, q, zero)
      qm_scr[parity, hd, tq:2 * tq, :] = jnp.where(lane >= DIFF_HEAD_DIM, q, zero)

  m_scr[...] = jnp.full(m_scr.shape, -jnp.inf, F32)
  acc_scr[...] = jnp.zeros(acc_scr.shape, F32)
  ones = jnp.ones((tq, HEAD_WIDTH), BF16)

  n_chunks = 2 * tq // chunk

  def pipeline_step(*, consume=None, produce=None, parity=None, causal=False,
                    diagonal=False):
    if consume is not None:
      v_start = pl.multiple_of(consume * tq, tq)
    if produce is not None:
      k_start = pl.multiple_of(produce * tq, tq)
    for hd in range(heads):
      head = slice(hd * HEAD_WIDTH, (hd + 1) * HEAD_WIDTH)
      if consume is not None:
        v1 = jnp.concatenate([v_ref[pl.ds(v_start, tq), head], ones], axis=1)
      if produce is not None:
        kb = k_ref[pl.ds(k_start, tq), head]
      for c in range(n_chunks):
        rows = slice(c * chunk, (c + 1) * chunk)
        if consume is not None:
          keys = (c * chunk) % tq + chunk if diagonal else tq
          s = s_scr[hd, rows, 0:keys]
          m_prev = m_scr[hd, rows, :]
          m_new = jnp.maximum(m_prev, jnp.max(s, axis=1, keepdims=True))
          alpha = jnp.exp2(m_prev - m_new)
          p = jnp.exp2(s - jnp.concatenate([m_new] * (keys // V7X_LANES), axis=1))
          acc_scr[hd, rows, :] = (
              jnp.concatenate([alpha, alpha], axis=1) * acc_scr[hd, rows, :]
              + _dot(p.astype(BF16), v1[0:keys]))
          m_scr[hd, rows, :] = m_new
        if produce is not None:
          s_new = lax.dot_general(qm_scr[parity, hd, rows, :], kb,
                                  (((1,), (1,)), ((), ())), preferred_element_type=F32)
          if causal:
            row = (lax.broadcasted_iota(jnp.int32, s_new.shape, 0) + c * chunk) % tq
            col = lax.broadcasted_iota(jnp.int32, s_new.shape, 1)
            s_new = jnp.where(col <= row, s_new, -jnp.inf)
          s_scr[hd, rows, :] = s_new

  @pl.when(i == 0)
  def _():
    stage_queries(0, 0)
    pipeline_step(produce=0, parity=0, causal=True)

  def body(j, carry):
    pipeline_step(consume=j, produce=j + 1, parity=par)
    return carry

  lax.fori_loop(0, i - 1, body, 0)

  @pl.when(i > 0)
  def _():
    pipeline_step(consume=i - 1, produce=i, parity=par, causal=True)

  stage_queries(jnp.minimum(i + 1, nq - 1), 1 - par)
  pipeline_step(consume=i, produce=0, parity=1 - par, diagonal=True)

  lam_v = lam_ref[...]
  lam = (jnp.exp(jnp.sum(lam_v[0:1] * lam_v[1:2], axis=1, keepdims=True))
         - jnp.exp(jnp.sum(lam_v[2:3] * lam_v[3:4], axis=1, keepdims=True))
         + lambda_init)
  for hd in range(heads):
    acc = acc_scr[hd]
    o_all = acc[:, :HEAD_WIDTH] / acc[:, HEAD_WIDTH:]
    o = o_all[:tq] - lam * o_all[tq:]
    o_ref[:, hd * HEAD_WIDTH:(hd + 1) * HEAD_WIDTH] = (
        _rms(o, g_ref[...], SUBLN_EPS) * (1.0 - lambda_init)).astype(o_ref.dtype)


def _flash_call(t, q, k, v, lam_v, subln_g, lambda_init, batch, seq):
  n, d = q.shape
  tq = t.q
  nq = seq // tq
  heads = t.heads_per_step
  width = heads * HEAD_WIDTH
  out_spec = pl.BlockSpec((tq, width), lambda b, h, i: (b * nq + i, h))
  seq_spec = pl.BlockSpec((seq, width), lambda b, h, i: (b, h))
  return pl.pallas_call(
      functools.partial(_flash_kernel, tq=tq, chunk=min(t.attn_chunk, tq), heads=heads,
                        lambda_init=lambda_init),
      grid=(batch, N_DIFF_HEADS // heads, nq),
      in_specs=[seq_spec, seq_spec, seq_spec,
                pl.BlockSpec(lam_v.shape, lambda b, h, i: (0, 0)),
                pl.BlockSpec(subln_g.shape, lambda b, h, i: (0, 0))],
      out_specs=out_spec,
      out_shape=jax.ShapeDtypeStruct((n, d), BF16),
      scratch_shapes=[pltpu.VMEM((2, heads, 2 * tq, HEAD_WIDTH), BF16),
                      pltpu.VMEM((heads, 2 * tq, HEAD_WIDTH), F32),
                      pltpu.VMEM((heads, 2 * tq, 2 * HEAD_WIDTH), F32),
                      pltpu.VMEM((heads, 2 * tq, tq), F32)],
      compiler_params=_params(t, ("parallel", "parallel", "arbitrary")),
      name="diff_flash",
  )(q, k, v, lam_v, subln_g)


def _ple(h, p, ln_g, w_gate, w_proj):
  gate = jax.nn.sigmoid(_dot(_rms(h, ln_g, NORM_EPS).astype(BF16), w_gate))
  return h + gate * _dot(p.astype(BF16), w_proj)


def _dense_tail_kernel(h_ref, o_ref, p_ref, wo_ref, ln_ffn_ref, wg_ref, wu_ref, wd_ref,
                       ln_ple_ref, wpg_ref, wpp_ref, out_ref, *, ff_chunks):
  h1 = h_ref[...] + _dot(o_ref[...], wo_ref[...])
  xn = _rms(h1, ln_ffn_ref[...], NORM_EPS).astype(BF16)
  d_ff = wg_ref.shape[1]
  cw = d_ff // ff_chunks
  y = None
  for c in range(ff_chunks):
    sl = slice(c * cw, (c + 1) * cw)
    g = _dot(xn, wg_ref[:, sl])
    u = _dot(xn, wu_ref[:, sl])
    part = _dot((g * jax.nn.sigmoid(g) * u).astype(BF16), wd_ref[sl, :])
    y = part if y is None else y + part
  h2 = h1 + y
  out_ref[...] = _ple(h2, p_ref[...], ln_ple_ref[...], wpg_ref[...], wpp_ref[...])


def _dense_tail_call(t, h, o, p, layer, wo, ln_ffn, wg, wu, wd, ln_ple, wpg, wpp):
  n, d = h.shape
  tm = t.tok
  d_ff = wg.shape[1]
  d_ple = p.shape[2]
  assert d_ff % (t.ff_chunks * V7X_LANES) == 0
  tok_spec = pl.BlockSpec((tm, d), lambda i: (i, 0))
  return pl.pallas_call(
      functools.partial(_dense_tail_kernel, ff_chunks=t.ff_chunks),
      grid=(n // tm,),
      in_specs=[tok_spec, tok_spec,
                pl.BlockSpec((None, tm, d_ple), lambda i: (layer, i, 0)),
                _resident((d, d)), _resident((1, d)), _resident((d, d_ff)),
                _resident((d, d_ff)), _resident((d_ff, d)), _resident((1, d)),
                _resident((d, d)), _resident((d_ple, d))],
      out_specs=tok_spec,
      out_shape=jax.ShapeDtypeStruct((n, d), F32),
      compiler_params=_params(t, ("parallel",)),
      name="dense_tail",
  )(h, o, p, wo, ln_ffn, wg, wu, wd, ln_ple, wpg, wpp)


def _rec_kernel(h_ref, ln_mix_ref, wx_ref, wy_ref, convw_ref, convb_ref, wgate_ref, ba_ref,
                bi_ref, lam_ref, wout_ref, ln_ffn_ref, wr_ref,
                h1_ref, xn2_ref, logit_ref,
                xs_scr, a_scr, b_scr, hs_scr, carry_scr):
  tt = pl.program_id(1)
  streams, ts, width = h_ref.shape
  pad = V7X_SUBLANES

  @pl.when(tt == 0)
  def _():
    xs_scr[:, 0:pad, :] = jnp.zeros((streams, pad, width), F32)
    carry_scr[...] = jnp.zeros(carry_scr.shape, F32)

  neg_lam = -lam_ref[...]
  softplus = jnp.maximum(neg_lam, 0.0) + jnp.log1p(jnp.exp(-jnp.abs(neg_lam)))
  cw = convw_ref[...]

  def gate_stage(st):
    x = h_ref[st]
    xn = _rms(x, ln_mix_ref[...], NORM_EPS).astype(BF16)
    yv = _dot(xn, wy_ref[...])
    yb = yv * jax.nn.sigmoid(
        yv * (2.0 * math.sqrt(2.0 / math.pi) * (1.0 + 0.044715 * (yv * yv))))
    xs_scr[st, pad:pad + ts, :] = _dot(xn, wx_ref[...])

    u = convb_ref[...] + cw[CONV_WIDTH - 1:CONV_WIDTH] * xs_scr[st, pad:pad + ts, :]
    for back in range(1, CONV_WIDTH):
      tap = CONV_WIDTH - 1 - back
      u = u + cw[tap:tap + 1] * xs_scr[st, pad - back:pad - back + ts, :]
    xs_scr[st, 0:pad, :] = xs_scr[st, ts:ts + pad, :]

    ub = u.astype(BF16)
    gates = [_dot(ub[:, nb * LRU_BLOCK:(nb + 1) * LRU_BLOCK], wgate_ref[nb])
             for nb in range(N_LRU_BLOCKS)]
    r = jax.nn.sigmoid(
        jnp.concatenate([g[:, :LRU_BLOCK] for g in gates], axis=1) + ba_ref[...])
    gate_i = jax.nn.sigmoid(
        jnp.concatenate([g[:, LRU_BLOCK:] for g in gates], axis=1) + bi_ref[...])
    log_a = (-LRU_C) * r * softplus
    a = jnp.exp(log_a)
    one_m_a2 = -jnp.tanh(log_a) * (a * a + 1.0)
    mult = jnp.where(one_m_a2 > 0.0, one_m_a2 * lax.rsqrt(one_m_a2), 0.0)
    a_scr[st] = a
    b_scr[st] = mult * gate_i * u
    return x, yb

  staged = [gate_stage(st) for st in range(streams)]

  row = lax.broadcasted_iota(jnp.int32, (V7X_SUBLANES, width), 0)

  def scan_stage(st):
    carry = jnp.broadcast_to(carry_scr[st], row.shape)
    for gi in range(ts // V7X_SUBLANES):
      rows = slice(gi * V7X_SUBLANES, (gi + 1) * V7X_SUBLANES)
      av = a_scr[st, rows, :]
      bv = b_scr[st, rows, :]
      for sh in (1, 2, 4):
        keep = row >= sh
        a_prev = jnp.where(keep, pltpu.roll(av, sh, 0), 1.0)
        b_prev = jnp.where(keep, pltpu.roll(bv, sh, 0), 0.0)
        bv = av * b_prev + bv
        av = av * a_prev
      hv = av * carry + bv
      hs_scr[st, rows, :] = hv
      carry = jnp.broadcast_to(hv[V7X_SUBLANES - 1:V7X_SUBLANES, :], hv.shape)
    carry_scr[st] = carry[0:1, :]

  for st in range(streams):
    x, yb = staged[st]
    scan_stage(st)
    h1 = x + _dot((hs_scr[st] * yb).astype(BF16), wout_ref[...])
    h1_ref[st] = h1
    xn2 = _rms(h1, ln_ffn_ref[...], NORM_EPS)
    x_hi = xn2.astype(BF16)
    xn2_ref[st] = x_hi
    x_lo = (xn2 - x_hi.astype(F32)).astype(BF16)
    parts = _dot(x_hi, wr_ref[...]) + _dot(x_lo, wr_ref[...])
    logit_ref[st] = parts[:, :V7X_LANES] + parts[:, V7X_LANES:]


def _rec_call(t, h, batch, seq, ln_mix, wx, wy, conv_w, conv_b, w_gates, ba, bi, lam, wout,
              ln_ffn, w_router):
  n, d = h.shape
  ts = t.tok
  streams = t.rec_streams if batch % t.rec_streams == 0 else 1

  def tok_spec(width):
    return pl.BlockSpec((streams, ts, width), lambda b, s: (b, s, 0))

  scr = pltpu.VMEM((streams, ts, d), F32)
  h1, xn2, logits = pl.pallas_call(
      _rec_kernel,
      grid=(batch // streams, seq // ts),
      in_specs=[tok_spec(d), _resident((1, d)), _resident((d, d)), _resident((d, d)),
                _resident(conv_w.shape), _resident((1, d)), _resident(w_gates.shape),
                _resident((1, d)), _resident((1, d)),
                _resident((1, d)), _resident((d, d)), _resident((1, d)),
                _resident(w_router.shape)],
      out_specs=[tok_spec(d), tok_spec(d), tok_spec(V7X_LANES)],
      out_shape=[jax.ShapeDtypeStruct((batch, seq, d), F32),
                 jax.ShapeDtypeStruct((batch, seq, d), BF16),
                 jax.ShapeDtypeStruct((batch, seq, V7X_LANES), F32)],
      scratch_shapes=[pltpu.VMEM((streams, ts + V7X_SUBLANES, d), F32), scr, scr, scr,
                      pltpu.VMEM((streams, 1, d), F32)],
      compiler_params=_params(t, ("parallel", "arbitrary")),
      name="recurrent_block",
  )(h.reshape(batch, seq, d), ln_mix, wx, wy, conv_w, conv_b, w_gates, ba, bi, lam, wout,
    ln_ffn, w_router)
  return h1.reshape(n, d), xn2.reshape(n, d), logits.reshape(n, V7X_LANES)


def _route_kernel(logit_ref, ids_ref, gates_ref, pos_t_ref, cnt_ref):
  lg = logit_ref[...]
  tr = lg.shape[0]
  lane = lax.broadcasted_iota(jnp.int32, lg.shape, 1)
  lg = jnp.where(lane < N_EXPERTS, lg, -jnp.inf)
  m1 = jnp.max(lg, axis=1, keepdims=True)
  e1 = jnp.min(jnp.where(lg == m1, lane, V7X_LANES), axis=1, keepdims=True)
  lg2 = jnp.where(lane == e1, -jnp.inf, lg)
  m2 = jnp.max(lg2, axis=1, keepdims=True)
  e2 = jnp.min(jnp.where(lg2 == m2, lane, V7X_LANES), axis=1, keepdims=True)
  ex = jnp.exp(m2 - m1)
  g1 = 1.0 / (1.0 + ex)
  g2 = ex / (1.0 + ex)

  chosen = jnp.where((lane == e1) | (lane == e2), 1.0, 0.0)
  r_i = lax.broadcasted_iota(jnp.int32, (tr, tr), 0)
  c_i = lax.broadcasted_iota(jnp.int32, (tr, tr), 1)
  earlier = jnp.where(c_i < r_i, 1.0, 0.0).astype(BF16)
  rank = _dot(earlier, chosen.astype(BF16))
  cnt = jnp.sum(chosen, axis=0, keepdims=True)
  run = jnp.ceil(cnt * (1.0 / V7X_SUBLANES)) * V7X_SUBLANES

  def position(e):
    return jnp.sum(jnp.where(lane < e, run, jnp.where(lane == e, rank, 0.0)),
                   axis=1, keepdims=True).astype(jnp.int32)

  pos1, pos2 = position(e1), position(e2)
  ids_ref[...] = jnp.where(lane == 0, e1, jnp.where(lane == 1, e2, jnp.where(
      lane == 2, pos1, jnp.where(lane == 3, pos2, 0))))
  gates_ref[...] = jnp.where(lane == 0, g1, jnp.where(lane == 1, g2, 0.0))
  pos_cols = jnp.where(lane == 0, pos1, jnp.where(lane == 1, pos2, 0))
  pos_t_ref[0] = jnp.transpose(pos_cols)[0:V7X_SUBLANES, :]
  cnt_ref[0] = jnp.broadcast_to(cnt, cnt_ref.shape[1:])


def _route_call(t, logits):
  n = logits.shape[0]
  tr = t.tok
  spec = pl.BlockSpec((tr, V7X_LANES), lambda i: (i, 0))
  return pl.pallas_call(
      _route_kernel,
      grid=(n // tr,),
      in_specs=[spec],
      out_specs=[spec, spec,
                 pl.BlockSpec((1, V7X_SUBLANES, tr), lambda i: (i, 0, 0)),
                 pl.BlockSpec((1, V7X_SUBLANES, V7X_LANES), lambda i: (i, 0, 0))],
      out_shape=[jax.ShapeDtypeStruct((n, V7X_LANES), jnp.int32),
                 jax.ShapeDtypeStruct((n, V7X_LANES), F32),
                 jax.ShapeDtypeStruct((n // tr, V7X_SUBLANES, tr), jnp.int32),
                 jax.ShapeDtypeStruct((n // tr, V7X_SUBLANES, V7X_LANES), F32)],
      compiler_params=_params(t, ("parallel",)),
      name="route_top2",
  )(logits)


def _run_copies(run_ref, slot_ref, tile, make_copy, max_rows):
  top_bit = max_rows.bit_length() - 1
  low_bit = V7X_SUBLANES.bit_length() - 1
  assert max_rows == 1 << top_bit
  tile_row = 0
  for e in range(N_EXPERTS):
    run = run_ref[tile * N_EXPERTS + e]
    slot_row = slot_ref[tile * N_EXPERTS + e]
    for k in range(top_bit, low_bit - 1, -1):
      done = (run >> (k + 1)) << (k + 1)

      @pl.when(((run >> k) & 1) == 1)
      def _():
        make_copy(pl.multiple_of(tile_row + done, V7X_SUBLANES),
                  pl.multiple_of(slot_row + done, V7X_SUBLANES),
                  1 << k).start(priority=(e + k) % 2)
    tile_row = tile_row + run


def _wait_rows(hbm, vmem_buf, rows, sem):
  rows = pl.multiple_of(rows, V7X_SUBLANES)
  pltpu.make_async_copy(hbm.at[pl.ds(0, rows), :], vmem_buf.at[pl.ds(0, rows), :], sem).wait()


def _dispatch_kernel(run_ref, slot_ref, total_ref, zero_ref, x_ref, pos_t_ref, xs_hbm,
                     y_scr, sems, *, n_tiles, expert_rows):
  i = pl.program_id(0)
  buf = i % 2

  @pl.when(i == 0)
  def _():
    y_scr[1, 0:expert_rows, :] = jnp.zeros((expert_rows, y_scr.shape[2]), y_scr.dtype)

    def zero_copy(e):
      first = pl.multiple_of(zero_ref[e], expert_rows)
      return pltpu.make_async_copy(y_scr.at[1, pl.ds(0, expert_rows), :],
                                   xs_hbm.at[pl.ds(first, expert_rows), :], sems.at[1])
    for e in range(N_EXPERTS):
      @pl.when(zero_ref[e] >= 0)
      def _():
        zero_copy(e).start()
    for e in range(N_EXPERTS):
      @pl.when(zero_ref[e] >= 0)
      def _():
        zero_copy(e).wait()

    def zero_block(b, carry):
      first = pl.multiple_of(b * expert_rows, expert_rows)
      cp = pltpu.make_async_copy(y_scr.at[1, pl.ds(0, expert_rows), :],
                                 xs_hbm.at[pl.ds(first, expert_rows), :], sems.at[1])
      cp.start()
      cp.wait()
      return carry

    lax.fori_loop(zero_ref[N_EXPERTS], xs_hbm.shape[0] // expert_rows, zero_block, 0)

  pos = pos_t_ref[0]
  order = lax.broadcasted_iota(jnp.int32, (y_scr.shape[1], x_ref.shape[0]), 0)
  select = jnp.where((order == pos[0:1, :]) | (order == pos[1:2, :]), 1.0, 0.0)
  y_scr[buf] = _dot(select.astype(BF16), x_ref[...])

  def make_copy(tile_row, slot_row, rows):
    return pltpu.make_async_copy(y_scr.at[buf, pl.ds(tile_row, rows), :],
                                 xs_hbm.at[pl.ds(slot_row, rows), :], sems.at[buf])

  _run_copies(run_ref, slot_ref, i, make_copy, x_ref.shape[0])

  @pl.when(i > 0)
  def _():
    _wait_rows(xs_hbm, y_scr.at[1 - buf], total_ref[i - 1], sems.at[1 - buf])

  @pl.when(i == n_tiles - 1)
  def _():
    _wait_rows(xs_hbm, y_scr.at[buf], total_ref[i], sems.at[buf])


def _dispatch_call(t, tile_run, tile_slot, tile_total, zero_rows, x, pos_t, n_slots):
  n, d = x.shape
  tm = t.tok
  n_tiles = n // tm
  grid_spec = pltpu.PrefetchScalarGridSpec(
      num_scalar_prefetch=4,
      grid=(n_tiles,),
      in_specs=[pl.BlockSpec((tm, d), lambda i, *_: (i, 0)),
                pl.BlockSpec((1, V7X_SUBLANES, tm), lambda i, *_: (i, 0, 0))],
      out_specs=pl.BlockSpec(memory_space=pl.ANY),
      scratch_shapes=[pltpu.VMEM((2, t.pair_rows, d), F32), pltpu.SemaphoreType.DMA((2,))],
  )
  return pl.pallas_call(
      functools.partial(_dispatch_kernel, n_tiles=n_tiles, expert_rows=t.expert),
      grid_spec=grid_spec,
      out_shape=jax.ShapeDtypeStruct((n_slots, d), F32),
      compiler_params=_params(t, ("arbitrary",)),
      name="moe_dispatch",
  )(tile_run, tile_slot, tile_total, zero_rows, x, pos_t)


def _expert_kernel(be_ref, used_ref, xs_ref, wg_ref, wu_ref, wd_ref, ys_ref, *, ff_chunks):
  del be_ref
  i = pl.program_id(0)

  @pl.when(i < used_ref[0])
  def _():
    x = xs_ref[...].astype(BF16)
    d_ff = wg_ref.shape[1]
    cw = d_ff // ff_chunks
    y = None
    for c in range(ff_chunks):
      sl = slice(c * cw, (c + 1) * cw)
      g = _dot(x, wg_ref[:, sl])
      u = _dot(x, wu_ref[:, sl])
      part = _dot((g * jax.nn.sigmoid(g) * u).astype(BF16), wd_ref[sl, :])
      y = part if y is None else y + part
    ys_ref[...] = y

  @pl.when(i >= used_ref[0])
  def _():
    ys_ref[...] = jnp.zeros(ys_ref.shape, ys_ref.dtype)


def _expert_call(t, block_expert, n_used, xs, wg, wu, wd, layer):
  n_slots, d = xs.shape
  tm = t.expert
  d_ff = wg.shape[3]
  assert d_ff % (t.ff_chunks * V7X_LANES) == 0
  row_spec = pl.BlockSpec((tm, d), lambda i, be, used: (i, 0))
  used_row_spec = pl.BlockSpec((tm, d), lambda i, be, used: (jnp.minimum(i, used[0] - 1), 0))
  once = pl.Buffered(1)
  grid_spec = pltpu.PrefetchScalarGridSpec(
      num_scalar_prefetch=2,
      grid=(n_slots // tm,),
      in_specs=[used_row_spec,
                pl.BlockSpec((None, None, d, d_ff),
                             lambda i, be, used: (layer, be[i], 0, 0), pipeline_mode=once),
                pl.BlockSpec((None, None, d, d_ff),
                             lambda i, be, used: (layer, be[i], 0, 0), pipeline_mode=once),
                pl.BlockSpec((None, None, d_ff, d),
                             lambda i, be, used: (layer, be[i], 0, 0), pipeline_mode=once)],
      out_specs=row_spec,
  )
  return pl.pallas_call(
      functools.partial(_expert_kernel, ff_chunks=t.ff_chunks),
      grid_spec=grid_spec,
      out_shape=jax.ShapeDtypeStruct((n_slots, d), F32),
      compiler_params=_params(t, ("arbitrary",)),
      name="moe_experts",
  )(block_expert, n_used, xs, wg, wu, wd)


def _combine_kernel(run_ref, slot_ref, total_ref, ys_hbm, h1_ref, ids_ref, gates_ref, p_ref,
                    ln_ple_ref, wpg_ref, wpp_ref, ln_final_ref, out_ref, y_scr, sems,
                    *, n_tiles, final_norm):
  s = pl.program_id(0)
  pairs = y_scr.shape[1]

  @pl.when(s < n_tiles)
  def _():
    buf = s % 2

    def make_copy(tile_row, slot_row, rows):
      return pltpu.make_async_copy(ys_hbm.at[pl.ds(slot_row, rows), :],
                                   y_scr.at[buf, pl.ds(tile_row, rows), :], sems.at[buf])

    _run_copies(run_ref, slot_ref, s, make_copy, h1_ref.shape[0])

  @pl.when(s > 0)
  def _():
    buf = (s - 1) % 2
    _wait_rows(ys_hbm, y_scr.at[buf], total_ref[s - 1], sems.at[buf])
    fetched = lax.broadcasted_iota(jnp.int32, (pairs, 1), 0) < total_ref[s - 1]
    y = jnp.where(fetched, y_scr[buf], 0.0).astype(BF16)
    ids = ids_ref[...]
    gates = gates_ref[...]
    order = lax.broadcasted_iota(jnp.int32, (ids.shape[0], pairs), 1)
    y1 = _dot(jnp.where(order == ids[:, 2:3], 1.0, 0.0).astype(BF16), y)
    y2 = _dot(jnp.where(order == ids[:, 3:4], 1.0, 0.0).astype(BF16), y)
    h2 = h1_ref[...] + (gates[:, 0:1] * y1 + gates[:, 1:2] * y2)
    h3 = _ple(h2, p_ref[...], ln_ple_ref[...], wpg_ref[...], wpp_ref[...])
    if final_norm:
      h3 = _rms(h3, ln_final_ref[...], NORM_EPS)
    out_ref[...] = h3


def _combine_call(t, tile_run, tile_slot, tile_total, ys, h1, ids, gates, p, layer, ln_ple,
                  wpg, wpp, ln_final, final_norm):
  n, d = h1.shape
  tm = t.tok
  n_tiles = n // tm
  d_ple = p.shape[2]
  prev = lambda s, *_: (jnp.maximum(s - 1, 0), 0)
  const = lambda s, *_: (0, 0)
  once = pl.Buffered(1)
  grid_spec = pltpu.PrefetchScalarGridSpec(
      num_scalar_prefetch=3,
      grid=(n_tiles + 1,),
      in_specs=[pl.BlockSpec(memory_space=pl.ANY),
                pl.BlockSpec((tm, d), prev), pl.BlockSpec((tm, V7X_LANES), prev),
                pl.BlockSpec((tm, V7X_LANES), prev),
                pl.BlockSpec((None, tm, d_ple),
                             lambda s, *_: (layer, jnp.maximum(s - 1, 0), 0)),
                pl.BlockSpec((1, d), const, pipeline_mode=once),
                pl.BlockSpec((d, d), const, pipeline_mode=once),
                pl.BlockSpec((d_ple, d), const, pipeline_mode=once),
                pl.BlockSpec((1, d), const, pipeline_mode=once)],
      out_specs=pl.BlockSpec((tm, d), prev),
      scratch_shapes=[pltpu.VMEM((2, t.pair_rows, d), F32), pltpu.SemaphoreType.DMA((2,))],
  )
  return pl.pallas_call(
      functools.partial(_combine_kernel, n_tiles=n_tiles, final_norm=final_norm),
      grid_spec=grid_spec,
      out_shape=jax.ShapeDtypeStruct((n, d), F32),
      compiler_params=_params(t, ("arbitrary",)),
      name="moe_combine_ple",
  )(tile_run, tile_slot, tile_total, ys, h1, ids, gates, p, ln_ple, wpg, wpp, ln_final)


def _rope_tables(seq):
  inv = 1.0 / (ROPE_THETA ** (jnp.arange(0, DIFF_HEAD_DIM, 2, dtype=F32) / DIFF_HEAD_DIM))
  ang = jnp.arange(seq, dtype=F32)[:, None] * inv[None, :]
  cos, sin = jnp.cos(ang), jnp.sin(ang)
  reps = HEAD_WIDTH // DIFF_HEAD_DIM
  cos_t = jnp.tile(jnp.concatenate([cos, cos], axis=1), (1, reps))
  sin_t = jnp.tile(jnp.concatenate([-sin, sin], axis=1), (1, reps))
  return cos_t, sin_t


def _slot_layout(t, cnt_f):
  tile_cnt = cnt_f[:, 0, :N_EXPERTS].astype(jnp.int32)
  tile_run = (tile_cnt + V7X_SUBLANES - 1) // V7X_SUBLANES * V7X_SUBLANES
  seg_len = jnp.sum(tile_run, axis=0)
  padded = (seg_len + t.expert - 1) // t.expert * t.expert
  seg_end = jnp.cumsum(padded)
  seg_start = seg_end - padded
  tile_slot = seg_start[None, :] + jnp.cumsum(tile_run, axis=0) - tile_run
  n_used = seg_end[-1:] // t.expert
  zero_rows = jnp.concatenate([jnp.where(padded > 0, seg_end - t.expert, -1), n_used])
  block_start = jnp.arange(t.n_expert_blocks, dtype=jnp.int32) * t.expert
  block_expert = jnp.minimum(
      jnp.searchsorted(seg_end, block_start, side='right'), N_EXPERTS - 1)
  i32 = lambda v: v.astype(jnp.int32)
  return (i32(tile_run.reshape(-1)), i32(tile_slot.reshape(-1)),
          i32(jnp.sum(tile_run, axis=1)), i32(zero_rows), i32(block_expert), i32(n_used))


def kernel(x, p, ln_mix, ln_ffn, ln_ple, ln_final, ple_w_proj, ple_w_gate, attn_wq, attn_wk, attn_wv, attn_wo, attn_lambda_q1, attn_lambda_k1, attn_lambda_q2, attn_lambda_k2, attn_subln, ffn_w_gate, ffn_w_up, ffn_w_down, lru_w_in_x, lru_w_in_y, lru_conv_w, lru_conv_b, lru_gate_a_w, lru_gate_a_b, lru_gate_x_w, lru_gate_x_b, lru_lambda, lru_w_out, moe_w_router, moe_w_gate, moe_w_up, moe_w_down):
  batch, seq, d = x.shape
  depth = p.shape[0]
  n = batch * seq
  t = _Tiles(batch, seq)
  row = lambda v: v.reshape(1, -1)
  bf = lambda w: w.astype(BF16)
  cos_t, sin_t = _rope_tables(seq)
  n_slots = t.n_expert_blocks * t.expert

  h = x.reshape(n, d)
  p_tok = p.reshape(depth, n, -1)
  moe_wg, moe_wu, moe_wd = bf(moe_w_gate), bf(moe_w_up), bf(moe_w_down)
  for i in range(depth):
    j = i // 2
    if i % 2 == 0:
      lambda_init = 0.8 - 0.6 * math.exp(-0.3 * i)
      q, k, v = _qkv_call(t, h, row(ln_mix[i]), bf(attn_wq[j]), bf(attn_wk[j]),
                          bf(attn_wv[j]), cos_t, sin_t, seq)
      lam_v = jnp.stack([attn_lambda_q1[j], attn_lambda_k1[j],
                         attn_lambda_q2[j], attn_lambda_k2[j]])
      o = _flash_call(t, q, k, v, lam_v, row(attn_subln[j]), lambda_init, batch, seq)
      h = _dense_tail_call(t, h, o, p_tok, i, bf(attn_wo[j]), row(ln_ffn[i]),
                           bf(ffn_w_gate[j]), bf(ffn_w_up[j]), bf(ffn_w_down[j]),
                           row(ln_ple[i]), bf(ple_w_gate[i]), bf(ple_w_proj[i]))
    else:
      w_r = jnp.pad(moe_w_router[j], ((0, 0), (0, V7X_LANES - N_EXPERTS)))
      w_r_hi = bf(w_r)
      w_router = jnp.concatenate([w_r_hi, bf(w_r - w_r_hi.astype(F32))], axis=1)
      w_gates = bf(jnp.concatenate([lru_gate_a_w[j], lru_gate_x_w[j]], axis=-1))
      h1, xn2, logits = _rec_call(
          t, h, batch, seq, row(ln_mix[i]), bf(lru_w_in_x[j]), bf(lru_w_in_y[j]),
          lru_conv_w[j], row(lru_conv_b[j]), w_gates, row(lru_gate_a_b[j]),
          row(lru_gate_x_b[j]), row(lru_lambda[j]), bf(lru_w_out[j]), row(ln_ffn[i]),
          w_router)
      ids, gates, pos_t, cnt = _route_call(t, logits)
      tile_run, tile_slot, tile_total, zero_rows, block_expert, n_used = _slot_layout(t, cnt)
      xs = _dispatch_call(t, tile_run, tile_slot, tile_total, zero_rows, xn2, pos_t, n_slots)
      ys = _expert_call(t, block_expert, n_used, xs, moe_wg, moe_wu, moe_wd, j)
      h = _combine_call(t, tile_run, tile_slot, tile_total, ys, h1, ids, gates, p_tok, i,
                        row(ln_ple[i]), bf(ple_w_gate[i]), bf(ple_w_proj[i]), row(ln_final),
                        final_norm=(i == depth - 1))
  if depth % 2 == 1:
    raise NotImplementedError("final norm is fused into the last (odd) layer")
  return h.reshape(batch, seq, d)
```
